```python
import math
import jax
import jax.numpy as jnp
from jax import lax
import numpy as np

D_MODEL = 1024
BATCH = 8
SEQ = 2048
DEPTH = 4
DEC_BATCH = 32
DEC_SEQ = 16
PAST_LEN = 1024

CHUNK = 64
Q_BLOCK = 128
EPS = 1e-6
D_FF = 4 * D_MODEL
H_A = 6
NOPE_D = 64
ROPE_D = 32
V_D = 64
QK_D = NOPE_D + ROPE_D
Q_LORA = 256
KV_LORA = 128
ROPE_BASE = 10000.0
H_B = 6
HD_B = 64
H_IDX = 8
D_IDX = 32
TOPK_MAX = 256
S5_CH = D_MODEL // 4
S5_GROUP = 16
S5_G = S5_CH // S5_GROUP
S5_P = 64
D_MIX = H_A * V_D + H_B * HD_B + S5_CH
IN_SIZES = (Q_LORA, KV_LORA, ROPE_D, H_B * HD_B, H_B * HD_B, H_B * HD_B, H_IDX * D_IDX, D_IDX, H_IDX, S5_CH)
IN_COLS = sum(IN_SIZES)

kernel_name = 'hybrid_mla_dsa_s5_stream_step'


def rmsnorm(x, g):
    xf = x.astype(jnp.float32)
    xf = xf * lax.rsqrt(jnp.mean(xf * xf, axis=-1, keepdims=True) + EPS)
    return xf.astype(x.dtype) * g


def rope(x, pos):
    half = ROPE_D // 2
    inv = ROPE_BASE ** (-jnp.arange(half, dtype=jnp.float32) / half)
    ang = pos.astype(jnp.float32)[:, None] * inv[None, :]
    cos = jnp.cos(ang)[None, :, None, :]
    sin = jnp.sin(ang)[None, :, None, :]
    xf = x.astype(jnp.float32)
    x1, x2 = xf[..., :half], xf[..., half:]
    return jnp.concatenate([x1 * cos - x2 * sin, x2 * cos + x1 * sin], axis=-1).astype(x.dtype)


def chunk_visible(qpos, kpos):
    return (kpos[None, :] // CHUNK) <= (qpos[:, None] // CHUNK)


def split_cols(z):
    outs = []
    off = 0
    for n in IN_SIZES:
        outs.append(z[..., off:off + n])
        off += n
    return outs


def map_query_blocks(fn, qpos, *qs):
    t = qpos.shape[0]
    if t > Q_BLOCK and t % Q_BLOCK == 0:
        nb = t // Q_BLOCK
        blocked = tuple(jnp.moveaxis(a.reshape((a.shape[0], nb, Q_BLOCK) + a.shape[2:]), 1, 0) for a in qs)
        out = lax.map(lambda args: fn(args[0], *args[1]), (qpos.reshape(nb, Q_BLOCK), blocked))
        out = jnp.moveaxis(out, 0, 1)
        return out.reshape((out.shape[0], t) + out.shape[3:])
    return fn(qpos, *qs)


def mla_mixer(cq, ckv_raw, kr_raw, qpos, past, w):
    b, t = cq.shape[:2]
    q = (rmsnorm(cq, w['mla_gq']) @ w['mla_wuq']).reshape(b, t, H_A, QK_D)
    q_nope = rmsnorm(q[..., :NOPE_D], w['mla_gqn'])
    q_rope = rope(rmsnorm(q[..., NOPE_D:], w['mla_gqr']), qpos)
    ckv = rmsnorm(ckv_raw, w['mla_gkv'])
    kr = rope(rmsnorm(kr_raw, w['mla_gkr'])[:, :, None, :], qpos)[:, :, 0, :]
    if past is None:
        ckv_all, kr_all = ckv, kr
    else:
        ckv_all = jnp.concatenate([past[0].astype(ckv.dtype), ckv], axis=1)
        kr_all = jnp.concatenate([past[1].astype(kr.dtype), kr], axis=1)
    s = ckv_all.shape[1]
    kpos = jnp.arange(s, dtype=jnp.int32)
    kv = (ckv_all @ w['mla_wukv']).reshape(b, s, H_A, NOPE_D + V_D)
    k_nope = rmsnorm(kv[..., :NOPE_D], w['mla_gkn'])
    v = kv[..., NOPE_D:]
    scale = QK_D ** -0.5

    def block(qp, qn, qr):
        sc = (jnp.einsum('bthd,bshd->bhts', qn, k_nope).astype(jnp.float32)
              + jnp.einsum('bthr,bsr->bhts', qr, kr_all).astype(jnp.float32)) * scale
        sc = jnp.where(chunk_visible(qp, kpos)[None, None], sc, -jnp.inf)
        p = jax.nn.softmax(sc, axis=-1).astype(v.dtype)
        return jnp.einsum('bhts,bshd->bthd', p, v)

    out = map_query_blocks(block, qpos, q_nope, q_rope)
    return out.reshape(b, t, H_A * V_D), ckv, kr


def dsa_mixer(bq, bk, bv, iq, ik, iw, qpos, past, w):
    b, t = bq.shape[:2]
    q = rmsnorm(bq.reshape(b, t, H_B, HD_B), w['dsa_gq'])
    k = rmsnorm(bk.reshape(b, t, H_B, HD_B), w['dsa_gk'])
    v = bv.reshape(b, t, H_B, HD_B)
    iq = iq.reshape(b, t, H_IDX, D_IDX)
    iw = iw * (H_IDX ** -0.5)
    if past is None:
        k_all, v_all, ik_all = k, v, ik
    else:
        k_all = jnp.concatenate([past[0].astype(k.dtype), k], axis=1)
        v_all = jnp.concatenate([past[1].astype(v.dtype), v], axis=1)
        ik_all = jnp.concatenate([past[2].astype(ik.dtype), ik], axis=1)
    s = k_all.shape[1]
    kpos = jnp.arange(s, dtype=jnp.int32)
    top = min(TOPK_MAX, s // 4)
    slopes = 2.0 ** (-8.0 * jnp.arange(1, H_B + 1, dtype=jnp.float32) / H_B)
    gather = jax.vmap(lambda rows, idx: rows[idx])

    def block(qp, qb, iqb, iwb):
        rel = jax.nn.relu(jnp.einsum('bthd,bsd->bths', iqb, ik_all).astype(jnp.float32) * (D_IDX ** -0.5))
        score = jnp.einsum('bth,bths->bts', iwb.astype(jnp.float32), rel)
        score = jnp.where(chunk_visible(qp, kpos)[None], score, -jnp.inf)
        _, idx = lax.top_k(score, top)
        valid = (idx // CHUNK) <= (qp[None, :, None] // CHUNK)
        kg = gather(k_all, idx)
        vg = gather(v_all, idx)
        sc = jnp.einsum('bthd,btkhd->bhtk', qb, kg).astype(jnp.float32) * (HD_B ** -0.5)
        dist = jnp.abs(qp[None, :, None] - idx).astype(jnp.float32)
        sc = sc - slopes[None, :, None, None] * dist[:, None]
        sc = jnp.where(valid[:, None], sc, -jnp.inf)
        p = jax.nn.softmax(sc, axis=-1).astype(vg.dtype)
        return jnp.einsum('bhtk,btkhd->bthd', p, vg)

    out = map_query_blocks(block, qpos, q, iq, iw)
    return out.reshape(b, t, H_B * HD_B), k, v, ik


def complex_affine_combine(e1, e2):
    a1r, a1i, b1r, b1i = e1
    a2r, a2i, b2r, b2i = e2
    return (a2r * a1r - a2i * a1i,
            a2r * a1i + a2i * a1r,
            a2r * b1r - a2i * b1i + b2r,
            a2r * b1i + a2i * b1r + b2i)


def s5_mixer(u, h0, w):
    b, t = u.shape[:2]
    f32 = jnp.float32
    ug = u.reshape(b, t, S5_G, S5_GROUP).astype(f32)
    ar = w['s5_a_re'].astype(f32)
    ai = w['s5_a_im'].astype(f32)
    dt = jnp.exp(w['s5_log_dt'].astype(f32))[:, None]
    mag = jnp.exp(dt * ar)
    ab_re = mag * jnp.cos(dt * ai)
    ab_im = mag * jnp.sin(dt * ai)
    den = ar * ar + ai * ai
    nr = ab_re - 1.0
    f_re = (nr * ar + ab_im * ai) / den
    f_im = (ab_im * ar - nr * ai) / den
    bu_re = jnp.einsum('btgc,gpc->btgp', ug, w['s5_b_re'].astype(f32))
    bu_im = jnp.einsum('btgc,gpc->btgp', ug, w['s5_b_im'].astype(f32))
    x_re = f_re * bu_re - f_im * bu_im
    x_im = f_re * bu_im + f_im * bu_re
    if h0 is not None:
        h_re = h0[0].astype(f32)
        h_im = h0[1].astype(f32)
        x_re = x_re.at[:, 0].add(ab_re * h_re - ab_im * h_im)
        x_im = x_im.at[:, 0].add(ab_re * h_im + ab_im * h_re)
    a_re = jnp.broadcast_to(ab_re, x_re.shape)
    a_im = jnp.broadcast_to(ab_im, x_im.shape)
    _, _, s_re, s_im = lax.associative_scan(complex_affine_combine, (a_re, a_im, x_re, x_im), axis=1)
    y = (jnp.einsum('btgp,gcp->btgc', s_re, w['s5_c_re'].astype(f32))
         - jnp.einsum('btgp,gcp->btgc', s_im, w['s5_c_im'].astype(f32))
         + w['s5_d'].astype(f32).reshape(S5_G, S5_GROUP) * ug)
    y = y.reshape(b, t, S5_CH).astype(u.dtype)
    g = y @ w['s5_w_glu'] + w['s5_b_glu']
    out = g[..., :S5_CH] * jax.nn.sigmoid(g[..., S5_CH:])
    return out, s_re[:, -1], s_im[:, -1]


def trunk_layer(x, c, qpos, past, w):
    mod = (jax.nn.silu(c) @ w['ada_w'] + w['ada_b'])[:, None, :]
    sh1, sc1, g1, sh2, sc2, g2 = jnp.split(mod, 6, axis=-1)
    h = rmsnorm(x, w['norm1_g']) * (1 + sc1) + sh1
    cq, ckv_raw, kr_raw, bq, bk, bv, iq, ik, iw, u = split_cols(h @ w['w_in'])
    if past is None:
        pa, pb, pc = None, None, None
    else:
        pa, pb, pc = past[0:2], past[2:5], past[5:7]
    a_out, ckv, kr = mla_mixer(cq, ckv_raw, kr_raw, qpos, pa, w)
    b_out, k, v, ikn = dsa_mixer(bq, bk, bv, iq, ik, iw, qpos, pb, w)
    c_out, s_re, s_im = s5_mixer(u, pc, w)
    mix = jnp.concatenate([a_out, b_out, c_out], axis=-1) @ w['w_out']
    x = x + g1 * mix
    h2 = rmsnorm(x, w['norm2_g']) * (1 + sc2) + sh2
    x = x + g2 * (jnp.square(jax.nn.relu(h2 @ w['ff_w1'])) @ w['ff_w2'])
    return x, (ckv, kr, k, v, ikn, s_re, s_im)


def setup_inputs(seed: int = 0) -> dict:
    key = jax.random.key(seed)
    ks = iter(jax.random.split(key, 48))
    f32 = jnp.float32

    def nrm(shape, scale=1.0):
        return scale * jax.random.normal(next(ks), shape, f32)

    def gain(n):
        return 1.0 + nrm((DEPTH, n), 0.1)

    n_idx = jnp.arange(S5_P, dtype=f32)
    return {
        'x_prompt': nrm((BATCH, SEQ, D_MODEL)),
        'x_sample': nrm((DEC_BATCH, DEC_SEQ, D_MODEL)),
        'c_prompt': nrm((BATCH, D_MODEL)),
        'c_sample': nrm((DEC_BATCH, D_MODEL)),
        'cache_mla_ckv': nrm((DEPTH, DEC_BATCH, PAST_LEN, KV_LORA)),
        'cache_mla_krope': nrm((DEPTH, DEC_BATCH, PAST_LEN, ROPE_D)),
        'cache_dsa_k': nrm((DEPTH, DEC_BATCH, PAST_LEN, H_B, HD_B)),
        'cache_dsa_v': nrm((DEPTH, DEC_BATCH, PAST_LEN, H_B, HD_B)),
        'cache_dsa_idxk': nrm((DEPTH, DEC_BATCH, PAST_LEN, D_IDX)),
        'state_s5_re': nrm((DEPTH, DEC_BATCH, S5_G, S5_P), 0.5),
        'state_s5_im': nrm((DEPTH, DEC_BATCH, S5_G, S5_P), 0.5),
        'ada_w': nrm((DEPTH, D_MODEL, 6 * D_MODEL), 0.5 * D_MODEL ** -0.5),
        'ada_b': nrm((DEPTH, 6 * D_MODEL), 0.01),
        'norm1_g': gain(D_MODEL),
        'norm2_g': gain(D_MODEL),
        'w_in': nrm((DEPTH, D_MODEL, IN_COLS), D_MODEL ** -0.5),
        'w_out': nrm((DEPTH, D_MIX, D_MODEL), D_MIX ** -0.5),
        'mla_gq': gain(Q_LORA),
        'mla_wuq': nrm((DEPTH, Q_LORA, H_A * QK_D), Q_LORA ** -0.5),
        'mla_gkv': gain(KV_LORA),
        'mla_wukv': nrm((DEPTH, KV_LORA, H_A * (NOPE_D + V_D)), KV_LORA ** -0.5),
        'mla_gqn': gain(NOPE_D),
        'mla_gqr': gain(ROPE_D),
        'mla_gkn': gain(NOPE_D),
        'mla_gkr': gain(ROPE_D),
        'dsa_gq': gain(HD_B),
        'dsa_gk': gain(HD_B),
        's5_a_re': -0.5 + nrm((DEPTH, S5_G, S5_P), 0.01),
        's5_a_im': math.pi * n_idx + nrm((DEPTH, S5_G, S5_P), 0.01),
        's5_b_re': nrm((DEPTH, S5_G, S5_P, S5_GROUP), S5_GROUP ** -0.5),
        's5_b_im': nrm((DEPTH, S5_G, S5_P, S5_GROUP), S5_GROUP ** -0.5),
        's5_c_re': nrm((DEPTH, S5_G, S5_GROUP, S5_P), S5_P ** -0.5),
        's5_c_im': nrm((DEPTH, S5_G, S5_GROUP, S5_P), S5_P ** -0.5),
        's5_d': nrm((DEPTH, S5_CH)),
        's5_log_dt': jax.random.uniform(next(ks), (DEPTH, S5_G), f32, math.log(1e-3), math.log(1e-1)),
        's5_w_glu': nrm((DEPTH, S5_CH, 2 * S5_CH), S5_CH ** -0.5),
        's5_b_glu': nrm((DEPTH, 2 * S5_CH), 0.01),
        'ff_w1': nrm((DEPTH, D_MODEL, D_FF), D_MODEL ** -0.5),
        'ff_w2': nrm((DEPTH, D_FF, D_MODEL), D_FF ** -0.5),
    }


def reference(x_prompt, x_sample, c_prompt, c_sample, cache_mla_ckv, cache_mla_krope, cache_dsa_k, cache_dsa_v,
              cache_dsa_idxk, state_s5_re, state_s5_im, ada_w, ada_b, norm1_g, norm2_g, w_in, w_out,
              mla_gq, mla_wuq, mla_gkv, mla_wukv, mla_gqn, mla_gqr, mla_gkn, mla_gkr, dsa_gq, dsa_gk,
              s5_a_re, s5_a_im, s5_b_re, s5_b_im, s5_c_re, s5_c_im, s5_d, s5_log_dt, s5_w_glu, s5_b_glu,
              ff_w1, ff_w2):
    past_len = cache_mla_ckv.shape[2]
    pos_p = jnp.arange(x_prompt.shape[1], dtype=jnp.int32)
    pos_s = past_len + jnp.arange(x_sample.shape[1], dtype=jnp.int32)
    xp, xs = x_prompt, x_sample
    new_p = [[] for _ in range(7)]
    new_s = [[] for _ in range(7)]
    for l in range(DEPTH):
        w = {
            'ada_w': ada_w[l], 'ada_b': ada_b[l], 'norm1_g': norm1_g[l], 'norm2_g': norm2_g[l],
            'w_in': w_in[l], 'w_out': w_out[l],
            'mla_gq': mla_gq[l], 'mla_wuq': mla_wuq[l], 'mla_gkv': mla_gkv[l], 'mla_wukv': mla_wukv[l],
            'mla_gqn': mla_gqn[l], 'mla_gqr': mla_gqr[l], 'mla_gkn': mla_gkn[l], 'mla_gkr': mla_gkr[l],
            'dsa_gq': dsa_gq[l], 'dsa_gk': dsa_gk[l],
            's5_a_re': s5_a_re[l], 's5_a_im': s5_a_im[l], 's5_b_re': s5_b_re[l], 's5_b_im': s5_b_im[l],
            's5_c_re': s5_c_re[l], 's5_c_im': s5_c_im[l], 's5_d': s5_d[l], 's5_log_dt': s5_log_dt[l],
            's5_w_glu': s5_w_glu[l], 's5_b_glu': s5_b_glu[l],
            'ff_w1': ff_w1[l], 'ff_w2': ff_w2[l],
        }
        xp, st_p = trunk_layer(xp, c_prompt, pos_p, None, w)
        past = (cache_mla_ckv[l], cache_mla_krope[l], cache_dsa_k[l], cache_dsa_v[l], cache_dsa_idxk[l],
                state_s5_re[l], state_s5_im[l])
        xs, st_s = trunk_layer(xs, c_sample, pos_s, past, w)
        for i in range(7):
            new_p[i].append(st_p[i])
            new_s[i].append(st_s[i])
    y_prompt = xp
    y_sample = xs
    new_mla_ckv_p = jnp.stack(new_p[0])
    new_mla_ckv_s = jnp.stack(new_s[0])
    new_mla_krope_p = jnp.stack(new_p[1])
    new_mla_krope_s = jnp.stack(new_s[1])
    new_dsa_k_p = jnp.stack(new_p[2])
    new_dsa_k_s = jnp.stack(new_s[2])
    new_dsa_v_p = jnp.stack(new_p[3])
    new_dsa_v_s = jnp.stack(new_s[3])
    new_dsa_idxk_p = jnp.stack(new_p[4])
    new_dsa_idxk_s = jnp.stack(new_s[4])
    new_s5_re_p = jnp.stack(new_p[5])
    new_s5_re_s = jnp.stack(new_s[5])
    new_s5_im_p = jnp.stack(new_p[6])
    new_s5_im_s = jnp.stack(new_s[6])
    return (y_prompt, y_sample, new_mla_ckv_p, new_mla_ckv_s, new_mla_krope_p, new_mla_krope_s,
            new_dsa_k_p, new_dsa_k_s, new_dsa_v_p, new_dsa_v_s, new_dsa_idxk_p, new_dsa_idxk_s,
            new_s5_re_p, new_s5_re_s, new_s5_im_p, new_s5_im_s)
```

```python
import functools
import math

import numpy as np
import jax
import jax.numpy as jnp
from jax import lax
from jax.experimental import pallas as pl
from jax.experimental.pallas import tpu as pltpu

D_MODEL = 1024
DEPTH = 4
CHUNK_SHIFT = 6
EPS = 1e-6
D_FF = 4 * D_MODEL
H_A = 6
NOPE_D = 64
ROPE_D = 32
V_D = 64
QK_D = NOPE_D + ROPE_D
Q_LORA = 256
KV_LORA = 128
ROPE_BASE = 10000.0
H_B = 6
HD_B = 64
H_IDX = 8
D_IDX = 32
TOPK_MAX = 256
S5_CH = D_MODEL // 4
S5_GROUP = 16
S5_G = S5_CH // S5_GROUP
S5_P = 64
S5_N = S5_G * S5_P
IN_SIZES = (Q_LORA, KV_LORA, ROPE_D, H_B * HD_B, H_B * HD_B, H_B * HD_B, H_IDX * D_IDX, D_IDX, H_IDX, S5_CH)

LANE = 128
BF = jnp.bfloat16
F32 = jnp.float32
NEG = -1e30
INT_MIN = -2 ** 31
VMEM_LIMIT = 56 * 1024 * 1024

SEG_CQ, SEG_CKV, SEG_KR, SEG_BQ, SEG_BK, SEG_BV, SEG_IQ, SEG_IK, SEG_IW, SEG_U = (
    0, 256, 384, 512, 896, 1280, 1664, 1920, 2048, 2176)
IN_COLS_PADDED = 2432


def _dot(a, b):
    return jnp.dot(a.astype(BF), b.astype(BF), preferred_element_type=F32)


def _dot_nt(a, b):
    return lax.dot_general(a.astype(BF), b.astype(BF), (((1,), (1,)), ((), ())), preferred_element_type=F32)


def _rs(x, n):
    return lax.rsqrt(jnp.sum(x * x, axis=-1, keepdims=True) * (1.0 / n) + EPS)


def _half_rs(x, lo, n_lo, n_hi):
    sq = x * x
    rs_lo = lax.rsqrt(jnp.sum(jnp.where(lo, sq, 0.0), axis=-1, keepdims=True) * (1.0 / n_lo) + EPS)
    rs_hi = lax.rsqrt(jnp.sum(jnp.where(lo, 0.0, sq), axis=-1, keepdims=True) * (1.0 / n_hi) + EPS)
    return jnp.where(lo, rs_lo, rs_hi)


def _rope(x, tab_ref):
    return x * tab_ref[0] + pltpu.roll(x, LANE - ROPE_D // 2, 1) * tab_ref[1] + pltpu.roll(x, ROPE_D // 2, 1) * tab_ref[2]


def _params(sem):
    return pltpu.CompilerParams(dimension_semantics=sem, vmem_limit_bytes=VMEM_LIMIT)


def _ada_body(c_ref, w_ref, b_ref, o_ref):
    c = c_ref[...]
    s = c * (1.0 / (1.0 + jnp.exp(-c)))
    o_ref[0] = _dot(s, w_ref[0]) + b_ref[0]


def _ada(c_all, ada_w, ada_b):
    n = c_all.shape[0]
    tn = 1536
    return pl.pallas_call(
        _ada_body,
        grid=(DEPTH, 6 * D_MODEL // tn),
        in_specs=[pl.BlockSpec((n, D_MODEL), lambda l, j: (0, 0)),
                  pl.BlockSpec((1, D_MODEL, tn), lambda l, j: (l, 0, j)),
                  pl.BlockSpec((1, 1, tn), lambda l, j: (l, 0, j))],
        out_specs=pl.BlockSpec((1, n, tn), lambda l, j: (l, 0, j)),
        out_shape=jax.ShapeDtypeStruct((DEPTH, n, 6 * D_MODEL), F32),
        compiler_params=_params(("arbitrary", "arbitrary")),
        name="ada_mod",
    )(c_all, ada_w, ada_b.reshape(DEPTH, 1, 6 * D_MODEL))


def _proj_body(x_ref, sc_ref, sh_ref, gn_ref, win_ref, gq_ref, wuq_ref, gqs_ref, tq_ref, gkv_ref, gkr_ref, tk_ref,
               gdq_ref, gdk_ref,
               q_o, ckv_o, kr_o, dq_o, dk_o, dkb_o, dv_o, dvb_o, iq_o, ik_o, iw_o, u_o):
    x = x_ref[0]
    tm = x.shape[0]
    xn = x * _rs(x, D_MODEL)
    h = xn * gn_ref[...] * (1.0 + sc_ref[0]) + sh_ref[0]
    z = _dot(h, win_ref[...])
    lo = lax.broadcasted_iota(jnp.int32, (tm, LANE), 1) < (LANE // 2)

    cq = z[:, SEG_CQ:SEG_CQ + Q_LORA]
    cqn = cq * _rs(cq, Q_LORA) * gq_ref[...]
    q = _dot(cqn, wuq_ref[...])
    for hh in range(H_A):
        qs = q[:, hh * LANE:(hh + 1) * LANE]
        qn = qs * _half_rs(qs, lo, NOPE_D, ROPE_D) * gqs_ref[...]
        q_o[0, :, hh * LANE:(hh + 1) * LANE] = (_rope(qn, tq_ref) * (QK_D ** -0.5)).astype(BF)

    ckv = z[:, SEG_CKV:SEG_CKV + KV_LORA]
    ckv_o[0] = ckv * _rs(ckv, KV_LORA) * gkv_ref[...]
    krs = z[:, SEG_KR:SEG_KR + LANE]
    krn = krs * _rs(krs, ROPE_D) * gkr_ref[...]
    kr_o[0] = _rope(krn, tk_ref)[:, 0:ROPE_D]

    for j in range(H_B // 2):
        bq = z[:, SEG_BQ + j * LANE:SEG_BQ + (j + 1) * LANE]
        dq_o[0, :, j * LANE:(j + 1) * LANE] = (
            bq * _half_rs(bq, lo, HD_B, HD_B) * gdq_ref[...] * (HD_B ** -0.5)).astype(BF)
        bk = z[:, SEG_BK + j * LANE:SEG_BK + (j + 1) * LANE]
        kn = bk * _half_rs(bk, lo, HD_B, HD_B) * gdk_ref[...]
        dk_o[0, :, j * LANE:(j + 1) * LANE] = kn
        dkb_o[0, :, j * LANE:(j + 1) * LANE] = kn.astype(BF)
    bv = z[:, SEG_BV:SEG_BV + H_B * HD_B]
    dv_o[0] = bv
    dvb_o[0] = bv.astype(BF)
    iq_o[0] = (z[:, SEG_IQ:SEG_IQ + H_IDX * D_IDX] * (D_IDX ** -0.5)).astype(BF)
    ik_o[0] = z[:, SEG_IK:SEG_IK + D_IDX]
    iw_o[0] = z[:, SEG_IW:SEG_IW + H_IDX] * (H_IDX ** -0.5)
    u_o[0] = z[:, SEG_U:SEG_U + S5_CH]


def _proj(x, sc, sh, tabq, tabk, w, tm):
    g, r, _ = x.shape
    per_token = sc.shape[1] != 1
    mod_spec = (pl.BlockSpec((1, tm, D_MODEL), lambda b, i: (b, i, 0)) if per_token
                else pl.BlockSpec((1, 1, D_MODEL), lambda b, i: (b, 0, 0)))
    const = lambda shape: pl.BlockSpec(shape, lambda b, i: tuple(0 for _ in shape))
    tab_spec = pl.BlockSpec((3, tm, LANE), lambda b, i: (0, i, 0))
    widths = (H_A * LANE, KV_LORA, ROPE_D, 384, 384, 384, 384, 384, 256, D_IDX, H_IDX, S5_CH)
    dtypes = (BF, F32, F32, BF, F32, BF, F32, BF, BF, F32, F32, F32)
    return pl.pallas_call(
        _proj_body,
        grid=(g, r // tm),
        in_specs=[pl.BlockSpec((1, tm, D_MODEL), lambda b, i: (b, i, 0)), mod_spec, mod_spec,
                  const((1, D_MODEL)), const((D_MODEL, IN_COLS_PADDED)), const((1, Q_LORA)),
                  const((Q_LORA, H_A * LANE)), const((1, LANE)), tab_spec, const((1, KV_LORA)), const((1, LANE)),
                  tab_spec, const((1, LANE)), const((1, LANE))],
        out_specs=[pl.BlockSpec((1, tm, n), lambda b, i: (b, i, 0)) for n in widths],
        out_shape=[jax.ShapeDtypeStruct((g, r, n), dt) for n, dt in zip(widths, dtypes)],
        compiler_params=_params(("arbitrary", "arbitrary")),
        name="in_proj",
    )(x, sc, sh, w["norm1_g"], w["w_in"], w["mla_gq"], w["mla_wuq"], w["mla_gq_slot"], tabq, w["mla_gkv"],
      w["mla_gkr"], tabk, w["dsa_gq"], w["dsa_gk"])


def _kvup_body(ckv_ref, kr_ref, wk_ref, wv_ref, gkn_ref, place_ref, k_o, v_o):
    ckv = ckv_ref[...]
    krp = jnp.dot(kr_ref[...], place_ref[...], precision=lax.Precision.HIGHEST, preferred_element_type=F32)
    kn = _dot(ckv, wk_ref[...])
    for hh in range(H_A):
        ks = kn[:, hh * LANE:(hh + 1) * LANE]
        k_o[:, hh * LANE:(hh + 1) * LANE] = (ks * _rs(ks, NOPE_D) * gkn_ref[...] + krp).astype(BF)
    v_o[...] = _dot(ckv, wv_ref[...]).astype(BF)


def _kvup(ckv, kr, w, tm):
    n = ckv.shape[0]
    const = lambda shape: pl.BlockSpec(shape, lambda i: tuple(0 for _ in shape))
    return pl.pallas_call(
        _kvup_body,
        grid=(n // tm,),
        in_specs=[pl.BlockSpec((tm, KV_LORA), lambda i: (i, 0)), pl.BlockSpec((tm, ROPE_D), lambda i: (i, 0)),
                  const((KV_LORA, H_A * LANE)), const((KV_LORA, H_A * V_D)), const((1, LANE)),
                  const((ROPE_D, LANE))],
        out_specs=[pl.BlockSpec((tm, H_A * LANE), lambda i: (i, 0)), pl.BlockSpec((tm, H_A * V_D), lambda i: (i, 0))],
        out_shape=[jax.ShapeDtypeStruct((n, H_A * LANE), BF), jax.ShapeDtypeStruct((n, H_A * V_D), BF)],
        compiler_params=_params(("arbitrary",)),
        name="mla_kv_up",
    )(ckv, kr, w["mla_wk"], w["mla_wv"], w["mla_gkn_slot"], w["kr_place"])


def _visible(tq, sp, q0, s_valid):
    qpos = q0 + lax.broadcasted_iota(jnp.int32, (tq, sp), 0)
    kpos = lax.broadcasted_iota(jnp.int32, (tq, sp), 1)
    vis = lax.shift_right_logical(kpos, CHUNK_SHIFT) <= lax.shift_right_logical(qpos, CHUNK_SHIFT)
    if s_valid < sp:
        vis = jnp.logical_and(vis, kpos < s_valid)
    return vis, qpos, kpos


def _gather_keys(past_ref, new_ref, all_ref, p, t):
    sp = all_ref.shape[0]
    all_ref[0:p, :] = past_ref[0].astype(all_ref.dtype)
    all_ref[p:p + t, :] = new_ref[0].astype(all_ref.dtype)
    if p + t < sp:
        all_ref[p + t:sp, :] = jnp.zeros((sp - p - t, all_ref.shape[1]), all_ref.dtype)
    return all_ref[...]


def _mla_body(p, t, tq, q_ref, k_ref, v_ref, *rest):
    if p:
        kp_ref, vp_ref, o_ref, kall, vall = rest
        k = _gather_keys(kp_ref, k_ref, kall, p, t)
        v = _gather_keys(vp_ref, v_ref, vall, p, t)
    else:
        (o_ref,) = rest
        k = k_ref[0]
        v = v_ref[0]
    sp = k.shape[0]
    q = q_ref[0]
    vis, _, _ = _visible(tq, sp, p + pl.program_id(1) * tq, p + t)
    lo = lax.broadcasted_iota(jnp.int32, (tq, LANE), 1) < (LANE // 2)
    for j in range(H_A // 2):
        vpair = v[:, j * LANE:(j + 1) * LANE]
        outs = []
        for e in range(2):
            hh = 2 * j + e
            s = _dot_nt(q[:, hh * LANE:(hh + 1) * LANE], k[:, hh * LANE:(hh + 1) * LANE])
            s = jnp.where(vis, s, NEG)
            pr = jnp.exp(s - jnp.max(s, axis=-1, keepdims=True))
            outs.append(_dot(pr, vpair) / jnp.sum(pr, axis=-1, keepdims=True))
        o_ref[0, :, j * LANE:(j + 1) * LANE] = jnp.where(lo, outs[0], outs[1]).astype(BF)


def _mla(q, k, v, kp, vp, tq):
    b, t, _ = q.shape
    p = 0 if kp is None else kp.shape[1]
    sp = -(-(p + t) // LANE) * LANE
    batch = lambda n, w: pl.BlockSpec((1, n, w), lambda bi, i: (bi, 0, 0))
    in_specs = [pl.BlockSpec((1, tq, H_A * LANE), lambda bi, i: (bi, i, 0)), batch(t, H_A * LANE), batch(t, H_A * V_D)]
    args = [q, k, v]
    scratch = []
    if p:
        in_specs += [batch(p, H_A * LANE), batch(p, H_A * V_D)]
        args += [kp, vp]
        scratch = [pltpu.VMEM((sp, H_A * LANE), BF), pltpu.VMEM((sp, H_A * V_D), BF)]
    return pl.pallas_call(
        functools.partial(_mla_body, p, t, tq),
        grid=(b, t // tq),
        in_specs=in_specs,
        out_specs=pl.BlockSpec((1, tq, H_A * V_D), lambda bi, i: (bi, i, 0)),
        out_shape=jax.ShapeDtypeStruct((b, t, H_A * V_D), BF),
        scratch_shapes=scratch,
        compiler_params=_params(("arbitrary", "arbitrary")),
        name="mla_attn",
    )(*args)


def _dsa_body(p, t, tq, top, q_ref, iq_ref, iw_ref, k_ref, v_ref, ik_ref, tile_ref, *rest):
    if p:
        kp_ref, vp_ref, ikp_ref, o_ref, key_ref, kall, vall, ikall = rest
        k = _gather_keys(kp_ref, k_ref, kall, p, t)
        v = _gather_keys(vp_ref, v_ref, vall, p, t)
        ik = _gather_keys(ikp_ref, ik_ref, ikall, p, t)
    else:
        o_ref, key_ref = rest
        k = k_ref[0]
        v = v_ref[0]
        ik = ik_ref[0]
    sp = k.shape[0]
    vis, qpos, kpos = _visible(tq, sp, p + pl.program_id(1) * tq, p + t)

    ik_tiled = _dot(ik, tile_ref[...]).astype(BF)
    iq = iq_ref[0]
    iw = iw_ref[0]
    head_of_lane = lax.shift_right_logical(lax.broadcasted_iota(jnp.int32, (tq, H_IDX * D_IDX), 1), 5)
    score = jnp.zeros((tq, sp), F32)
    for hh in range(H_IDX):
        qh = jnp.where(head_of_lane == hh, iq, jnp.zeros_like(iq))
        score = score + iw[:, hh:hh + 1] * jnp.maximum(_dot_nt(qh, ik_tiled), 0.0)
    score = jnp.where(vis, score, -jnp.inf) + 0.0

    bits = lax.bitcast_convert_type(score, jnp.int32)
    key_ref[...] = jnp.where(bits >= 0, bits, bits ^ 0x7FFFFFFF)
    kk = float(top)

    def count_ge(cand):
        return jnp.sum(jnp.where(key_ref[...] >= cand, 1.0, 0.0), axis=-1, keepdims=True)

    thr0 = jnp.where(count_ge(jnp.zeros((tq, 1), jnp.int32)) >= kk, 0, INT_MIN).astype(jnp.int32)

    def bit_step(i, thr):
        cand = thr + lax.shift_left(jnp.int32(1), 30 - i)
        return jnp.where(count_ge(cand) >= kk, cand, thr)

    thr = lax.fori_loop(0, 31, bit_step, thr0)
    sel = jnp.logical_and(key_ref[...] >= thr, vis)
    bias = jnp.where(sel, 0.0, NEG)
    dist = jnp.abs(qpos - kpos).astype(F32)

    q = q_ref[0]
    lo = lax.broadcasted_iota(jnp.int32, (tq, LANE), 1) < (LANE // 2)
    for j in range(H_B // 2):
        qpair = q[:, j * LANE:(j + 1) * LANE]
        kpair = k[:, j * LANE:(j + 1) * LANE]
        vpair = v[:, j * LANE:(j + 1) * LANE]
        outs = []
        for e in range(2):
            hh = 2 * j + e
            slope = 2.0 ** (-8.0 * (hh + 1) / H_B)
            qh = jnp.where(lo if e == 0 else jnp.logical_not(lo), qpair, jnp.zeros_like(qpair))
            s = _dot_nt(qh, kpair) - slope * dist + bias
            pr = jnp.exp(s - jnp.max(s, axis=-1, keepdims=True))
            outs.append(_dot(pr, vpair) / jnp.sum(pr, axis=-1, keepdims=True))
        o_ref[0, :, j * LANE:(j + 1) * LANE] = jnp.where(lo, outs[0], outs[1]).astype(BF)


def _dsa(q, iq, iw, k, v, ik, kp, vp, ikp, w, tq):
    b, t, _ = q.shape
    p = 0 if kp is None else kp.shape[1]
    sp = -(-(p + t) // LANE) * LANE
    top = min(TOPK_MAX, (p + t) // 4)
    tile = lambda n, wd: pl.BlockSpec((1, n, wd), lambda bi, i: (bi, i, 0))
    batch = lambda n, wd: pl.BlockSpec((1, n, wd), lambda bi, i: (bi, 0, 0))
    in_specs = [tile(tq, 384), tile(tq, 256), tile(tq, H_IDX), batch(t, 384), batch(t, 384), batch(t, D_IDX),
                pl.BlockSpec((D_IDX, H_IDX * D_IDX), lambda bi, i: (0, 0))]
    args = [q, iq, iw, k, v, ik, w["idx_tile"]]
    scratch = [pltpu.VMEM((tq, sp), jnp.int32)]
    if p:
        in_specs += [batch(p, 384), batch(p, 384), batch(p, D_IDX)]
        args += [kp, vp, ikp]
        scratch += [pltpu.VMEM((sp, 384), BF), pltpu.VMEM((sp, 384), BF), pltpu.VMEM((sp, D_IDX), F32)]
    return pl.pallas_call(
        functools.partial(_dsa_body, p, t, tq, top),
        grid=(b, t // tq),
        in_specs=in_specs,
        out_specs=tile(tq, 384),
        out_shape=jax.ShapeDtypeStruct((b, t, 384), BF),
        scratch_shapes=scratch,
        compiler_params=_params(("arbitrary", "arbitrary")),
        name="dsa_attn",
    )(*args)


def _s5_body(has_h0, nb, tt, u_ref, are_ref, aim_ref, ldt_ref, bre_ref, bim_ref, cre_ref, cim_ref, d_ref, wg_ref,
             bg_ref, *rest):
    if has_h0:
        h0re_ref, h0im_ref, o_ref, sre_o, sim_o, xre, xim = rest
    else:
        o_ref, sre_o, sim_o, xre, xim = rest

    @pl.when(pl.program_id(0) == 0)
    def _():
        if has_h0:
            sre_o[...] = h0re_ref[...]
            sim_o[...] = h0im_ref[...]
        else:
            sre_o[...] = jnp.zeros_like(sre_o)
            sim_o[...] = jnp.zeros_like(sim_o)

    ar = are_ref[...]
    ai = aim_ref[...]
    dt = jnp.exp(ldt_ref[...])
    mag = jnp.exp(dt * ar)
    ab_re = mag * jnp.cos(dt * ai)
    ab_im = mag * jnp.sin(dt * ai)
    den = ar * ar + ai * ai
    nr = ab_re - 1.0
    f_re = (nr * ar + ab_im * ai) / den
    f_im = (ab_im * ar - nr * ai) / den

    u = u_ref[...].reshape(nb * tt, S5_CH)
    bu_re = _dot(u, bre_ref[...])
    bu_im = _dot(u, bim_ref[...])
    x_re = f_re * bu_re - f_im * bu_im
    x_im = f_re * bu_im + f_im * bu_re
    ncol = S5_N // LANE
    for c in range(ncol):
        xre[c] = x_re[:, c * LANE:(c + 1) * LANE]
        xim[c] = x_im[:, c * LANE:(c + 1) * LANE]

    a_re = jnp.broadcast_to(ab_re, (nb, S5_N))
    a_im = jnp.broadcast_to(ab_im, (nb, S5_N))

    def step(ti, carry):
        s_re, s_im = carry
        rows = pl.ds(ti, nb, stride=tt)
        n_re = a_re * s_re - a_im * s_im + jnp.concatenate([xre[c, rows, :] for c in range(ncol)], axis=-1)
        n_im = a_re * s_im + a_im * s_re + jnp.concatenate([xim[c, rows, :] for c in range(ncol)], axis=-1)
        for c in range(ncol):
            xre[c, rows, :] = n_re[:, c * LANE:(c + 1) * LANE]
            xim[c, rows, :] = n_im[:, c * LANE:(c + 1) * LANE]
        return n_re, n_im

    s_re, s_im = lax.fori_loop(0, tt, step, (sre_o[...], sim_o[...]))
    sre_o[...] = s_re
    sim_o[...] = s_im

    y = d_ref[...] * u
    for c in range(ncol):
        y = y + _dot(xre[c], cre_ref[c * LANE:(c + 1) * LANE, :]) - _dot(xim[c], cim_ref[c * LANE:(c + 1) * LANE, :])
    g = _dot(y, wg_ref[...]) + bg_ref[...]
    out = g[:, 0:S5_CH] * (1.0 / (1.0 + jnp.exp(-g[:, S5_CH:2 * S5_CH])))
    o_ref[...] = out.reshape(nb, tt, S5_CH).astype(BF)


def _s5(u, h0, w, tt):
    nb, t, _ = u.shape
    const = lambda shape: pl.BlockSpec(shape, lambda i: tuple(0 for _ in shape))
    in_specs = [pl.BlockSpec((nb, tt, S5_CH), lambda i: (0, i, 0)), const((1, S5_N)), const((1, S5_N)),
                const((1, S5_N)), const((S5_CH, S5_N)), const((S5_CH, S5_N)), const((S5_N, S5_CH)),
                const((S5_N, S5_CH)), const((1, S5_CH)), const((S5_CH, 2 * S5_CH)), const((1, 2 * S5_CH))]
    args = [u, w["s5_a_re"], w["s5_a_im"], w["s5_log_dt"], w["s5_bre"], w["s5_bim"], w["s5_cre"], w["s5_cim"],
            w["s5_d"], w["s5_w_glu"], w["s5_b_glu"]]
    if h0 is not None:
        in_specs += [const((nb, S5_N)), const((nb, S5_N))]
        args += [h0[0], h0[1]]
    return pl.pallas_call(
        functools.partial(_s5_body, h0 is not None, nb, tt),
        grid=(t // tt,),
        in_specs=in_specs,
        out_specs=[pl.BlockSpec((nb, tt, S5_CH), lambda i: (0, i, 0)), const((nb, S5_N)), const((nb, S5_N))],
        out_shape=[jax.ShapeDtypeStruct((nb, t, S5_CH), BF), jax.ShapeDtypeStruct((nb, S5_N), F32),
                   jax.ShapeDtypeStruct((nb, S5_N), F32)],
        scratch_shapes=[pltpu.VMEM((S5_N // LANE, nb * tt, LANE), F32), pltpu.VMEM((S5_N // LANE, nb * tt, LANE), F32)],
        compiler_params=_params(("arbitrary",)),
        name="s5_scan",
    )(*args)


def _out_body(a_ref, b_ref, c_ref, x_ref, g1_ref, sc_ref, sh_ref, g2_ref, gn_ref, wa_ref, wb_ref, wc_ref, w1_ref,
              w2_ref, o_ref):
    mix = _dot(a_ref[0], wa_ref[...]) + _dot(b_ref[0], wb_ref[...]) + _dot(c_ref[0], wc_ref[...])
    x1 = x_ref[0] + g1_ref[0] * mix
    h2 = (x1 * _rs(x1, D_MODEL) * gn_ref[...] * (1.0 + sc_ref[0]) + sh_ref[0]).astype(BF)
    ff = jnp.zeros_like(x1)
    for c in range(D_FF // D_MODEL):
        hid = jnp.maximum(_dot(h2, w1_ref[:, c * D_MODEL:(c + 1) * D_MODEL]), 0.0)
        ff = ff + _dot(hid * hid, w2_ref[c * D_MODEL:(c + 1) * D_MODEL, :])
    o_ref[0] = x1 + g2_ref[0] * ff


def _out(a, b, c, x, g1, sc, sh, g2, w, tm):
    g, r, _ = x.shape
    per_token = sc.shape[1] != 1
    mod_spec = (pl.BlockSpec((1, tm, D_MODEL), lambda bi, i: (bi, i, 0)) if per_token
                else pl.BlockSpec((1, 1, D_MODEL), lambda bi, i: (bi, 0, 0)))
    tile = lambda n: pl.BlockSpec((1, tm, n), lambda bi, i: (bi, i, 0))
    const = lambda shape: pl.BlockSpec(shape, lambda bi, i: tuple(0 for _ in shape), pipeline_mode=pl.Buffered(1))
    return pl.pallas_call(
        _out_body,
        grid=(g, r // tm),
        in_specs=[tile(384), tile(384), tile(S5_CH), tile(D_MODEL), mod_spec, mod_spec, mod_spec, mod_spec,
                  const((1, D_MODEL)), const((384, D_MODEL)), const((384, D_MODEL)), const((S5_CH, D_MODEL)),
                  const((D_MODEL, D_FF)), const((D_FF, D_MODEL))],
        out_specs=tile(D_MODEL),
        out_shape=jax.ShapeDtypeStruct((g, r, D_MODEL), F32),
        compiler_params=_params(("arbitrary", "arbitrary")),
        name="out_mlp",
    )(a, b, c, x, g1, sc, sh, g2, w["norm2_g"], w["w_out_a"], w["w_out_b"], w["w_out_c"], w["ff_w1"], w["ff_w2"])


def _in_col_index():
    offs = np.cumsum((0,) + IN_SIZES)
    idx = np.full((IN_COLS_PADDED,), -1, np.int64)
    starts = (SEG_CQ, SEG_CKV, SEG_KR, SEG_BQ, SEG_BK, SEG_BV, SEG_IQ, SEG_IK, SEG_IW, SEG_U)
    for s, o, n in zip(starts, offs[:-1], IN_SIZES):
        idx[s:s + n] = o + np.arange(n)
    return idx


def _take_cols(wm, idx):
    return jnp.where(jnp.asarray(idx >= 0), jnp.take(wm, jnp.asarray(np.maximum(idx, 0)), axis=-1), 0.0)


def _slot_pad(v, lo, n):
    return jnp.pad(v, ((0, 0), (lo, LANE - lo - n)))


def _prep_weights(p):
    f = {}
    f["w_in"] = _take_cols(p["w_in"], _in_col_index()).astype(BF)
    qidx = np.full((H_A * LANE,), -1, np.int64)
    kidx = np.full((H_A * LANE,), -1, np.int64)
    vidx = np.zeros((H_A * V_D,), np.int64)
    for hh in range(H_A):
        qidx[hh * LANE:hh * LANE + QK_D] = hh * QK_D + np.arange(QK_D)
        kidx[hh * LANE:hh * LANE + NOPE_D] = hh * (NOPE_D + V_D) + np.arange(NOPE_D)
        vidx[hh * V_D:(hh + 1) * V_D] = hh * (NOPE_D + V_D) + NOPE_D + np.arange(V_D)
    f["mla_wuq"] = _take_cols(p["mla_wuq"], qidx).astype(BF)
    f["mla_wk"] = _take_cols(p["mla_wukv"], kidx).astype(BF)
    f["mla_wv"] = _take_cols(p["mla_wukv"], vidx).astype(BF)
    f["mla_gq_slot"] = jnp.concatenate([p["mla_gqn"], p["mla_gqr"], jnp.zeros((DEPTH, LANE - QK_D), F32)], axis=-1)
    f["mla_gkn_slot"] = _slot_pad(p["mla_gkn"], 0, NOPE_D)
    f["mla_gkr"] = _slot_pad(p["mla_gkr"], 0, ROPE_D)
    f["dsa_gq"] = jnp.concatenate([p["dsa_gq"], p["dsa_gq"]], axis=-1)
    f["dsa_gk"] = jnp.concatenate([p["dsa_gk"], p["dsa_gk"]], axis=-1)
    for name in ("norm1_g", "norm2_g", "mla_gq", "mla_gkv", "s5_d", "s5_b_glu"):
        f[name] = p[name]
    f["w_out_a"] = p["w_out"][:, 0:384].astype(BF)
    f["w_out_b"] = p["w_out"][:, 384:768].astype(BF)
    f["w_out_c"] = p["w_out"][:, 768:1024].astype(BF)
    f["ff_w1"] = p["ff_w1"].astype(BF)
    f["ff_w2"] = p["ff_w2"].astype(BF)
    f["s5_w_glu"] = p["s5_w_glu"].astype(BF)
    eye = jnp.eye(S5_G, dtype=F32)
    f["s5_bre"] = jnp.einsum("lgpc,gh->lgchp", p["s5_b_re"], eye).reshape(DEPTH, S5_CH, S5_N).astype(BF)
    f["s5_bim"] = jnp.einsum("lgpc,gh->lgchp", p["s5_b_im"], eye).reshape(DEPTH, S5_CH, S5_N).astype(BF)
    f["s5_cre"] = jnp.einsum("lgcp,gh->lgphc", p["s5_c_re"], eye).reshape(DEPTH, S5_N, S5_CH).astype(BF)
    f["s5_cim"] = jnp.einsum("lgcp,gh->lgphc", p["s5_c_im"], eye).reshape(DEPTH, S5_N, S5_CH).astype(BF)
    f["s5_a_re"] = p["s5_a_re"].reshape(DEPTH, S5_N)
    f["s5_a_im"] = p["s5_a_im"].reshape(DEPTH, S5_N)
    f["s5_log_dt"] = jnp.repeat(p["s5_log_dt"], S5_P, axis=-1)
    return f


def _layer_weights(f, l):
    w = {k: (v[l] if v.ndim == 3 else v[l][None, :]) for k, v in f.items()}
    place = np.zeros((ROPE_D, LANE), np.float32)
    place[np.arange(ROPE_D), NOPE_D + np.arange(ROPE_D)] = 1.0
    w["kr_place"] = jnp.asarray(place)
    tile = np.zeros((D_IDX, H_IDX * D_IDX), np.float32)
    for hh in range(H_IDX):
        tile[np.arange(D_IDX), hh * D_IDX + np.arange(D_IDX)] = 1.0
    w["idx_tile"] = jnp.asarray(tile, dtype=BF)
    return w


def _rope_tables(pos, lo):
    half = ROPE_D // 2
    inv = ROPE_BASE ** (-jnp.arange(half, dtype=F32) / half)
    ang = pos.astype(F32)[:, None] * inv[None, :]
    cos, sin = jnp.cos(ang), jnp.sin(ang)
    n = pos.shape[0]
    z = lambda w_: jnp.zeros((n, w_), F32)
    tail = LANE - lo - ROPE_D
    c = jnp.concatenate([jnp.ones((n, lo), F32), cos, cos, z(tail)], axis=-1)
    s1 = jnp.concatenate([z(lo), -sin, z(half), z(tail)], axis=-1)
    s2 = jnp.concatenate([z(lo), z(half), sin, z(tail)], axis=-1)
    return jnp.stack([c, s1, s2])


def _layer(x, mods, past, w, pos, per_token_mod, tiles):
    b, t, _ = x.shape
    tm, tq_mla, tq_dsa, tt = tiles
    sh1, sc1, g1, sh2, sc2, g2 = mods
    tabq = _rope_tables(pos, NOPE_D)
    tabk = _rope_tables(pos, 0)
    if per_token_mod:
        xf = x.reshape(1, b * t, D_MODEL)
        exp = lambda m: jnp.broadcast_to(m, (b, t, D_MODEL)).reshape(1, b * t, D_MODEL)
        sh1, sc1, g1, sh2, sc2, g2 = (exp(m) for m in mods)
        tabq = jnp.tile(tabq, (1, b, 1))
        tabk = jnp.tile(tabk, (1, b, 1))
    else:
        xf = x
    outs = _proj(xf, sc1, sh1, tabq, tabk, w, tm)
    q, ckv, kr, dq, dk, dkb, dv, dvb, iq, ik, iw, u = (o.reshape(b, t, o.shape[-1]) for o in outs)

    kvt = min(512, b * t)
    kn, vn = _kvup(ckv.reshape(b * t, KV_LORA), kr.reshape(b * t, ROPE_D), w, kvt)
    kn = kn.reshape(b, t, H_A * LANE)
    vn = vn.reshape(b, t, H_A * V_D)
    if past is None:
        a_out = _mla(q, kn, vn, None, None, tq_mla)
        b_out = _dsa(dq, iq, iw, dkb, dvb, ik, None, None, None, w, tq_dsa)
        h0 = None
    else:
        p_ckv, p_kr, p_k, p_v, p_ik, p_sre, p_sim = past
        plen = p_ckv.shape[1]
        kp, vp = _kvup(p_ckv.reshape(b * plen, KV_LORA), p_kr.reshape(b * plen, ROPE_D), w, min(512, b * plen))
        a_out = _mla(q, kn, vn, kp.reshape(b, plen, H_A * LANE), vp.reshape(b, plen, H_A * V_D), tq_mla)
        b_out = _dsa(dq, iq, iw, dkb, dvb, ik, p_k.reshape(b, plen, 384), p_v.reshape(b, plen, 384), p_ik, w, tq_dsa)
        h0 = (p_sre.reshape(b, S5_N), p_sim.reshape(b, S5_N))
    c_out, s_re, s_im = _s5(u, h0, w, tt)

    shp = (1, b * t, -1) if per_token_mod else (b, t, -1)
    y = _out(a_out.reshape(shp), b_out.reshape(shp), c_out.reshape(shp), xf, g1, sc2, sh2, g2, w, tm)
    state = (ckv, kr, dk.reshape(b, t, H_B, HD_B), dv.reshape(b, t, H_B, HD_B), ik,
             s_re.reshape(b, S5_G, S5_P), s_im.reshape(b, S5_G, S5_P))
    return y.reshape(b, t, D_MODEL), state


def kernel(x_prompt, x_sample, c_prompt, c_sample, cache_mla_ckv, cache_mla_krope, cache_dsa_k, cache_dsa_v, cache_dsa_idxk, state_s5_re, state_s5_im, ada_w, ada_b, norm1_g, norm2_g, w_in, w_out, mla_gq, mla_wuq, mla_gkv, mla_wukv, mla_gqn, mla_gqr, mla_gkn, mla_gkr, dsa_gq, dsa_gk, s5_a_re, s5_a_im, s5_b_re, s5_b_im, s5_c_re, s5_c_im, s5_d, s5_log_dt, s5_w_glu, s5_b_glu, ff_w1, ff_w2):
    params = dict(norm1_g=norm1_g, norm2_g=norm2_g, w_in=w_in, w_out=w_out, mla_gq=mla_gq, mla_wuq=mla_wuq,
                  mla_gkv=mla_gkv, mla_wukv=mla_wukv, mla_gqn=mla_gqn, mla_gqr=mla_gqr, mla_gkn=mla_gkn,
                  mla_gkr=mla_gkr, dsa_gq=dsa_gq, dsa_gk=dsa_gk, s5_a_re=s5_a_re, s5_a_im=s5_a_im, s5_b_re=s5_b_re,
                  s5_b_im=s5_b_im, s5_c_re=s5_c_re, s5_c_im=s5_c_im, s5_d=s5_d, s5_log_dt=s5_log_dt,
                  s5_w_glu=s5_w_glu, s5_b_glu=s5_b_glu, ff_w1=ff_w1, ff_w2=ff_w2)
    f = _prep_weights(params)
    nbp, tp, _ = x_prompt.shape
    nbs, ts, _ = x_sample.shape
    past_len = cache_mla_ckv.shape[2]
    mod = _ada(jnp.concatenate([c_prompt, c_sample], axis=0), ada_w, ada_b)
    pos_p = jnp.arange(tp, dtype=jnp.int32)
    pos_s = past_len + jnp.arange(ts, dtype=jnp.int32)
    xp, xs = x_prompt, x_sample
    new_p = [[] for _ in range(7)]
    new_s = [[] for _ in range(7)]
    for l in range(DEPTH):
        w = _layer_weights(f, l)
        mods = [mod[l][:, None, i * D_MODEL:(i + 1) * D_MODEL] for i in range(6)]
        xp, st_p = _layer(xp, [m[:nbp] for m in mods], None, w, pos_p, False,
                          (min(512, tp), min(256, tp), min(128, tp), min(128, tp)))
        past = (cache_mla_ckv[l], cache_mla_krope[l], cache_dsa_k[l], cache_dsa_v[l], cache_dsa_idxk[l],
                state_s5_re[l], state_s5_im[l])
        xs, st_s = _layer(xs, [m[nbp:] for m in mods], past, w, pos_s, True, (nbs * ts, ts, ts, ts))
        for i in range(7):
            new_p[i].append(st_p[i])
            new_s[i].append(st_s[i])
    outs = [xp, xs]
    for i in range(7):
        outs.append(jnp.stack(new_p[i]))
        outs.append(jnp.stack(new_s[i]))
    return tuple(outs)
```

```python
import functools
import math

import numpy as np
import jax
import jax.numpy as jnp
from jax import lax
from jax.experimental import pallas as pl
from jax.experimental.pallas import tpu as pltpu

D_MODEL = 1024
DEPTH = 4
CHUNK_SHIFT = 6
EPS = 1e-6
D_FF = 4 * D_MODEL
H_A = 6
NOPE_D = 64
ROPE_D = 32
V_D = 64
QK_D = NOPE_D + ROPE_D
Q_LORA = 256
KV_LORA = 128
ROPE_BASE = 10000.0
H_B = 6
HD_B = 64
H_IDX = 8
D_IDX = 32
TOPK_MAX = 256
S5_CH = D_MODEL // 4
S5_GROUP = 16
S5_G = S5_CH // S5_GROUP
S5_P = 64
S5_N = S5_G * S5_P
IN_SIZES = (Q_LORA, KV_LORA, ROPE_D, H_B * HD_B, H_B * HD_B, H_B * HD_B, H_IDX * D_IDX, D_IDX, H_IDX, S5_CH)

LANE = 128
BF = jnp.bfloat16
F32 = jnp.float32
NEG = -1e30
INT_MIN = -2 ** 31
VMEM_LIMIT = 56 * 1024 * 1024
CAUSAL_SEG = 256

SEG_CQ, SEG_CKV, SEG_KR, SEG_BQ, SEG_BK, SEG_BV, SEG_IQ, SEG_IK, SEG_IW, SEG_U = (
    0, 256, 384, 512, 896, 1280, 1664, 1920, 2048, 2176)
IN_COLS_PADDED = 2432


def _dot(a, b):
    return jnp.dot(a.astype(BF), b.astype(BF), preferred_element_type=F32)


def _dot_nt(a, b):
    return lax.dot_general(a.astype(BF), b.astype(BF), (((1,), (1,)), ((), ())), preferred_element_type=F32)


def _rs(x, n):
    return lax.rsqrt(jnp.sum(x * x, axis=-1, keepdims=True) * (1.0 / n) + EPS)


def _half_rs(x, lo, n_lo, n_hi):
    sq = x * x
    rs_lo = lax.rsqrt(jnp.sum(jnp.where(lo, sq, 0.0), axis=-1, keepdims=True) * (1.0 / n_lo) + EPS)
    rs_hi = lax.rsqrt(jnp.sum(jnp.where(lo, 0.0, sq), axis=-1, keepdims=True) * (1.0 / n_hi) + EPS)
    return jnp.where(lo, rs_lo, rs_hi)


def _rope(x, tab_ref):
    return x * tab_ref[0] + pltpu.roll(x, LANE - ROPE_D // 2, 1) * tab_ref[1] + pltpu.roll(x, ROPE_D // 2, 1) * tab_ref[2]


def _params(sem):
    return pltpu.CompilerParams(dimension_semantics=sem, vmem_limit_bytes=VMEM_LIMIT)


def _ada_body(c_ref, w_ref, b_ref, o_ref):
    c = c_ref[...]
    s = c * (1.0 / (1.0 + jnp.exp(-c)))
    o_ref[0] = _dot(s, w_ref[0]) + b_ref[0]


def _ada(c_all, ada_w, ada_b):
    n = c_all.shape[0]
    tn = 1536
    return pl.pallas_call(
        _ada_body,
        grid=(DEPTH, 6 * D_MODEL // tn),
        in_specs=[pl.BlockSpec((n, D_MODEL), lambda l, j: (0, 0)),
                  pl.BlockSpec((1, D_MODEL, tn), lambda l, j: (l, 0, j)),
                  pl.BlockSpec((1, 1, tn), lambda l, j: (l, 0, j))],
        out_specs=pl.BlockSpec((1, n, tn), lambda l, j: (l, 0, j)),
        out_shape=jax.ShapeDtypeStruct((DEPTH, n, 6 * D_MODEL), F32),
        compiler_params=_params(("arbitrary", "arbitrary")),
        name="ada_mod",
    )(c_all, ada_w, ada_b.reshape(DEPTH, 1, 6 * D_MODEL))


def _proj_body(x_ref, sc_ref, sh_ref, gn_ref, win_ref, gq_ref, wuq_ref, gqs_ref, tq_ref, gkv_ref, gkr_ref, tk_ref,
               gdq_ref, gdk_ref, tile_ref,
               q_o, ckv_o, kr_o, dq_o, dk_o, dkb_o, dv_o, dvb_o, iq_o, ik_o, ikt_o, iw_o, u_o):
    x = x_ref[0]
    tm = x.shape[0]
    xn = x * _rs(x, D_MODEL)
    h = xn * gn_ref[...] * (1.0 + sc_ref[0]) + sh_ref[0]
    z = _dot(h, win_ref[...])
    lo = lax.broadcasted_iota(jnp.int32, (tm, LANE), 1) < (LANE // 2)

    cq = z[:, SEG_CQ:SEG_CQ + Q_LORA]
    cqn = cq * _rs(cq, Q_LORA) * gq_ref[...]
    q = _dot(cqn, wuq_ref[...])
    for hh in range(H_A):
        qs = q[:, hh * LANE:(hh + 1) * LANE]
        qn = qs * _half_rs(qs, lo, NOPE_D, ROPE_D) * gqs_ref[...]
        q_o[0, :, hh * LANE:(hh + 1) * LANE] = (_rope(qn, tq_ref) * (QK_D ** -0.5)).astype(BF)

    ckv = z[:, SEG_CKV:SEG_CKV + KV_LORA]
    ckv_o[0] = ckv * _rs(ckv, KV_LORA) * gkv_ref[...]
    krs = z[:, SEG_KR:SEG_KR + LANE]
    krn = krs * _rs(krs, ROPE_D) * gkr_ref[...]
    kr_o[0] = _rope(krn, tk_ref)[:, 0:ROPE_D]

    for j in range(H_B // 2):
        bq = z[:, SEG_BQ + j * LANE:SEG_BQ + (j + 1) * LANE]
        dq_o[0, :, j * LANE:(j + 1) * LANE] = (
            bq * _half_rs(bq, lo, HD_B, HD_B) * gdq_ref[...] * (HD_B ** -0.5)).astype(BF)
        bk = z[:, SEG_BK + j * LANE:SEG_BK + (j + 1) * LANE]
        kn = bk * _half_rs(bk, lo, HD_B, HD_B) * gdk_ref[...]
        dk_o[0, :, j * LANE:(j + 1) * LANE] = kn
        dkb_o[0, :, j * LANE:(j + 1) * LANE] = kn.astype(BF)
    bv = z[:, SEG_BV:SEG_BV + H_B * HD_B]
    dv_o[0] = bv
    dvb_o[0] = bv.astype(BF)
    iq_o[0] = (z[:, SEG_IQ:SEG_IQ + H_IDX * D_IDX] * (D_IDX ** -0.5)).astype(BF)
    ik = z[:, SEG_IK:SEG_IK + D_IDX]
    ik_o[0] = ik
    ikt_o[0] = _dot(ik, tile_ref[...]).astype(BF)
    iw_o[0] = z[:, SEG_IW:SEG_IW + H_IDX] * (H_IDX ** -0.5)
    u_o[0] = z[:, SEG_U:SEG_U + S5_CH]


def _proj(x, sc, sh, tabq, tabk, w, tm):
    g, r, _ = x.shape
    per_token = sc.shape[1] != 1
    mod_spec = (pl.BlockSpec((1, tm, D_MODEL), lambda b, i: (b, i, 0)) if per_token
                else pl.BlockSpec((1, 1, D_MODEL), lambda b, i: (b, 0, 0)))
    const = lambda shape: pl.BlockSpec(shape, lambda b, i: tuple(0 for _ in shape))
    tab_spec = pl.BlockSpec((3, tm, LANE), lambda b, i: (0, i, 0))
    widths = (H_A * LANE, KV_LORA, ROPE_D, 384, 384, 384, 384, 384, 256, D_IDX, H_IDX * D_IDX, H_IDX, S5_CH)
    dtypes = (BF, F32, F32, BF, F32, BF, F32, BF, BF, F32, BF, F32, F32)
    return pl.pallas_call(
        _proj_body,
        grid=(g, r // tm),
        in_specs=[pl.BlockSpec((1, tm, D_MODEL), lambda b, i: (b, i, 0)), mod_spec, mod_spec,
                  const((1, D_MODEL)), const((D_MODEL, IN_COLS_PADDED)), const((1, Q_LORA)),
                  const((Q_LORA, H_A * LANE)), const((1, LANE)), tab_spec, const((1, KV_LORA)), const((1, LANE)),
                  tab_spec, const((1, LANE)), const((1, LANE)), const((D_IDX, H_IDX * D_IDX))],
        out_specs=[pl.BlockSpec((1, tm, n), lambda b, i: (b, i, 0)) for n in widths],
        out_shape=[jax.ShapeDtypeStruct((g, r, n), dt) for n, dt in zip(widths, dtypes)],
        compiler_params=_params(("arbitrary", "arbitrary")),
        name="in_proj",
    )(x, sc, sh, w["norm1_g"], w["w_in"], w["mla_gq"], w["mla_wuq"], w["mla_gq_slot"], tabq, w["mla_gkv"],
      w["mla_gkr"], tabk, w["dsa_gq"], w["dsa_gk"], w["idx_tile"])


def _kvup_body(ckv_ref, kr_ref, wk_ref, wv_ref, gkn_ref, place_ref, k_o, v_o):
    ckv = ckv_ref[...]
    krp = jnp.dot(kr_ref[...], place_ref[...], precision=lax.Precision.HIGHEST, preferred_element_type=F32)
    kn = _dot(ckv, wk_ref[...])
    for hh in range(H_A):
        ks = kn[:, hh * LANE:(hh + 1) * LANE]
        k_o[:, hh * LANE:(hh + 1) * LANE] = (ks * _rs(ks, NOPE_D) * gkn_ref[...] + krp).astype(BF)
    v_o[...] = _dot(ckv, wv_ref[...]).astype(BF)


def _kvup(ckv, kr, w, tm):
    n = ckv.shape[0]
    const = lambda shape: pl.BlockSpec(shape, lambda i: tuple(0 for _ in shape))
    return pl.pallas_call(
        _kvup_body,
        grid=(n // tm,),
        in_specs=[pl.BlockSpec((tm, KV_LORA), lambda i: (i, 0)), pl.BlockSpec((tm, ROPE_D), lambda i: (i, 0)),
                  const((KV_LORA, H_A * LANE)), const((KV_LORA, H_A * V_D)), const((1, LANE)),
                  const((ROPE_D, LANE))],
        out_specs=[pl.BlockSpec((tm, H_A * LANE), lambda i: (i, 0)), pl.BlockSpec((tm, H_A * V_D), lambda i: (i, 0))],
        out_shape=[jax.ShapeDtypeStruct((n, H_A * LANE), BF), jax.ShapeDtypeStruct((n, H_A * V_D), BF)],
        compiler_params=_params(("arbitrary",)),
        name="mla_kv_up",
    )(ckv, kr, w["mla_wk"], w["mla_wv"], w["mla_gkn_slot"], w["kr_place"])


def _visible(tq, sp, q0, s_valid):
    qpos = q0 + lax.broadcasted_iota(jnp.int32, (tq, sp), 0)
    kpos = lax.broadcasted_iota(jnp.int32, (tq, sp), 1)
    vis = lax.shift_right_logical(kpos, CHUNK_SHIFT) <= lax.shift_right_logical(qpos, CHUNK_SHIFT)
    if s_valid < sp:
        vis = jnp.logical_and(vis, kpos < s_valid)
    return vis, qpos, kpos


def _gather_keys(past_ref, new_ref, all_ref, p, t):
    sp = all_ref.shape[0]
    all_ref[0:p, :] = past_ref[0].astype(all_ref.dtype)
    all_ref[p:p + t, :] = new_ref[0].astype(all_ref.dtype)
    if p + t < sp:
        all_ref[p + t:sp, :] = jnp.zeros((sp - p - t, all_ref.shape[1]), all_ref.dtype)
    return all_ref[...]


def _causal_variants(t, tq, compute):
    seg = min(CAUSAL_SEG, t)
    i = pl.program_id(1)
    for sg in range(t // seg):
        first, last = sg * seg // tq, (sg + 1) * seg // tq
        pl.when(jnp.logical_and(i >= first, i < last))(functools.partial(compute, (sg + 1) * seg))


def _mla_body(p, t, tq, q_ref, k_ref, v_ref, *rest):
    if p:
        kp_ref, vp_ref, o_ref, kall, vall = rest
        k_all = _gather_keys(kp_ref, k_ref, kall, p, t)
        v_all = _gather_keys(vp_ref, v_ref, vall, p, t)
        _mla_compute(p, t, tq, q_ref, lambda n: k_all, lambda n: v_all, o_ref, k_all.shape[0])
    else:
        (o_ref,) = rest
        _causal_variants(t, tq, functools.partial(
            _mla_compute, p, t, tq, q_ref, lambda n: k_ref[0, 0:n, :], lambda n: v_ref[0, 0:n, :], o_ref))


def _mla_compute(p, t, tq, q_ref, load_k, load_v, o_ref, svis):
    k = load_k(svis)
    v = load_v(svis)
    q = q_ref[0]
    vis, _, _ = _visible(tq, svis, p + pl.program_id(1) * tq, p + t)
    lo = lax.broadcasted_iota(jnp.int32, (tq, LANE), 1) < (LANE // 2)
    for j in range(H_A // 2):
        vpair = v[:, j * LANE:(j + 1) * LANE]
        outs = []
        for e in range(2):
            hh = 2 * j + e
            s = _dot_nt(q[:, hh * LANE:(hh + 1) * LANE], k[:, hh * LANE:(hh + 1) * LANE])
            s = jnp.where(vis, s, NEG)
            pr = jnp.exp(s - jnp.max(s, axis=-1, keepdims=True))
            outs.append(_dot(pr, vpair) / jnp.sum(pr, axis=-1, keepdims=True))
        o_ref[0, :, j * LANE:(j + 1) * LANE] = jnp.where(lo, outs[0], outs[1]).astype(BF)


def _mla(q, k, v, kp, vp, tq):
    b, t, _ = q.shape
    p = 0 if kp is None else kp.shape[1]
    sp = -(-(p + t) // LANE) * LANE
    batch = lambda n, w: pl.BlockSpec((1, n, w), lambda bi, i: (bi, 0, 0))
    in_specs = [pl.BlockSpec((1, tq, H_A * LANE), lambda bi, i: (bi, i, 0)), batch(t, H_A * LANE), batch(t, H_A * V_D)]
    args = [q, k, v]
    scratch = []
    if p:
        in_specs += [batch(p, H_A * LANE), batch(p, H_A * V_D)]
        args += [kp, vp]
        scratch = [pltpu.VMEM((sp, H_A * LANE), BF), pltpu.VMEM((sp, H_A * V_D), BF)]
    return pl.pallas_call(
        functools.partial(_mla_body, p, t, tq),
        grid=(b, t // tq),
        in_specs=in_specs,
        out_specs=pl.BlockSpec((1, tq, H_A * V_D), lambda bi, i: (bi, i, 0)),
        out_shape=jax.ShapeDtypeStruct((b, t, H_A * V_D), BF),
        scratch_shapes=scratch,
        compiler_params=_params(("arbitrary", "arbitrary")),
        name="mla_attn",
    )(*args)


def _count16(ref, cand, ncol):
    rows = ref.shape[0]
    c16 = jnp.broadcast_to(cand, (rows, LANE)).astype(jnp.int16)
    one = jnp.ones((rows, LANE), BF)
    zero = jnp.zeros((rows, LANE), BF)
    acc = zero
    for c in range(ncol):
        acc = acc + jnp.where(ref[:, c * LANE:(c + 1) * LANE] >= c16, one, zero)
    return jnp.sum(acc.astype(F32), axis=-1, keepdims=True)


def _kth_largest_key(key, hi_ref, lo_ref, kk):
    rows, n = key.shape
    ncol = n // LANE
    hi = lax.shift_right_arithmetic(key, 16)
    hi_ref[:, 0:n] = hi.astype(jnp.int16)
    t_hi = jnp.where(_count16(hi_ref, jnp.zeros((rows, 1), jnp.int32), ncol) >= kk, 0, -32768).astype(jnp.int32)

    def hi_step(i, th):
        cand = th + lax.shift_left(jnp.int32(1), 14 - i)
        return jnp.where(_count16(hi_ref, cand, ncol) >= kk, cand, th)

    t_hi = lax.fori_loop(0, 15, hi_step, t_hi)
    above = jnp.sum(jnp.where(hi > t_hi, 1.0, 0.0), axis=-1, keepdims=True)
    lo_ref[:, 0:n] = jnp.where(hi == t_hi, (key & 0xFFFF) - 32768, -32768).astype(jnp.int16)

    def lo_step(i, tl):
        cand = tl + lax.shift_left(jnp.int32(1), 15 - i)
        return jnp.where(_count16(lo_ref, cand, ncol) >= kk - above, cand, tl)

    t_lo = lax.fori_loop(0, 16, lo_step, jnp.full((rows, 1), -32768, jnp.int32))
    return t_hi * 65536 + (t_lo + 32768)


def _dsa_body(p, t, tq, top, q_ref, iq_ref, iw_ref, k_ref, v_ref, ik_ref, tile_ref, *rest):
    if p:
        kp_ref, vp_ref, ikp_ref, o_ref, hi_ref, lo_ref, kall, vall, ikall = rest
        k_all = _gather_keys(kp_ref, k_ref, kall, p, t)
        v_all = _gather_keys(vp_ref, v_ref, vall, p, t)
        ikt_all = _dot(_gather_keys(ikp_ref, ik_ref, ikall, p, t), tile_ref[...]).astype(BF)
        _dsa_compute(p, t, tq, top, q_ref, iq_ref, iw_ref, lambda n: k_all, lambda n: v_all, lambda n: ikt_all,
                     o_ref, hi_ref, lo_ref, k_all.shape[0])
    else:
        o_ref, hi_ref, lo_ref = rest
        _causal_variants(t, tq, functools.partial(
            _dsa_compute, p, t, tq, top, q_ref, iq_ref, iw_ref, lambda n: k_ref[0, 0:n, :], lambda n: v_ref[0, 0:n, :],
            lambda n: ik_ref[0, 0:n, :], o_ref, hi_ref, lo_ref))


def _dsa_compute(p, t, tq, top, q_ref, iq_ref, iw_ref, load_k, load_v, load_ikt, o_ref, hi_ref, lo_ref, svis):
    k = load_k(svis)
    v = load_v(svis)
    ik_tiled = load_ikt(svis)
    vis, qpos, kpos = _visible(tq, svis, p + pl.program_id(1) * tq, p + t)

    iq = iq_ref[0]
    iw = iw_ref[0]
    head_of_lane = lax.shift_right_logical(lax.broadcasted_iota(jnp.int32, (tq, H_IDX * D_IDX), 1), 5)
    score = jnp.zeros((tq, svis), F32)
    for hh in range(H_IDX):
        qh = jnp.where(head_of_lane == hh, iq, jnp.zeros_like(iq))
        score = score + iw[:, hh:hh + 1] * jnp.maximum(_dot_nt(qh, ik_tiled), 0.0)
    score = jnp.where(vis, score, -jnp.inf) + 0.0

    bits = lax.bitcast_convert_type(score, jnp.int32)
    key = jnp.where(bits >= 0, bits, bits ^ 0x7FFFFFFF)
    thr = _kth_largest_key(key, hi_ref, lo_ref, float(top))
    sel = jnp.logical_and(key >= thr, vis)
    bias = jnp.where(sel, 0.0, NEG)
    dist = jnp.abs(qpos - kpos).astype(F32)

    q = q_ref[0]
    lo = lax.broadcasted_iota(jnp.int32, (tq, LANE), 1) < (LANE // 2)
    for j in range(H_B // 2):
        qpair = q[:, j * LANE:(j + 1) * LANE]
        kpair = k[:, j * LANE:(j + 1) * LANE]
        vpair = v[:, j * LANE:(j + 1) * LANE]
        outs = []
        for e in range(2):
            hh = 2 * j + e
            slope = 2.0 ** (-8.0 * (hh + 1) / H_B)
            qh = jnp.where(lo if e == 0 else jnp.logical_not(lo), qpair, jnp.zeros_like(qpair))
            s = _dot_nt(qh, kpair) - slope * dist + bias
            pr = jnp.exp(s - jnp.max(s, axis=-1, keepdims=True))
            outs.append(_dot(pr, vpair) / jnp.sum(pr, axis=-1, keepdims=True))
        o_ref[0, :, j * LANE:(j + 1) * LANE] = jnp.where(lo, outs[0], outs[1]).astype(BF)


def _dsa(q, iq, iw, k, v, ik, kp, vp, ikp, w, tq):
    b, t, _ = q.shape
    p = 0 if kp is None else kp.shape[1]
    sp = -(-(p + t) // LANE) * LANE
    top = min(TOPK_MAX, (p + t) // 4)
    tile = lambda n, wd: pl.BlockSpec((1, n, wd), lambda bi, i: (bi, i, 0))
    batch = lambda n, wd: pl.BlockSpec((1, n, wd), lambda bi, i: (bi, 0, 0))
    in_specs = [tile(tq, 384), tile(tq, 256), tile(tq, H_IDX), batch(t, 384), batch(t, 384), batch(t, ik.shape[-1]),
                pl.BlockSpec((D_IDX, H_IDX * D_IDX), lambda bi, i: (0, 0))]
    args = [q, iq, iw, k, v, ik, w["idx_tile"]]
    scratch = [pltpu.VMEM((tq, sp), jnp.int16), pltpu.VMEM((tq, sp), jnp.int16)]
    if p:
        in_specs += [batch(p, 384), batch(p, 384), batch(p, D_IDX)]
        args += [kp, vp, ikp]
        scratch += [pltpu.VMEM((sp, 384), BF), pltpu.VMEM((sp, 384), BF), pltpu.VMEM((sp, D_IDX), F32)]
    return pl.pallas_call(
        functools.partial(_dsa_body, p, t, tq, top),
        grid=(b, t // tq),
        in_specs=in_specs,
        out_specs=tile(tq, 384),
        out_shape=jax.ShapeDtypeStruct((b, t, 384), BF),
        scratch_shapes=scratch,
        compiler_params=_params(("arbitrary", "arbitrary")),
        name="dsa_attn",
    )(*args)


def _s5_body(has_h0, nb, tt, u_ref, are_ref, aim_ref, ldt_ref, bre_ref, bim_ref, cre_ref, cim_ref, d_ref, wg_ref,
             bg_ref, *rest):
    if has_h0:
        h0re_ref, h0im_ref, o_ref, sre_o, sim_o, xre, xim = rest
    else:
        o_ref, sre_o, sim_o, xre, xim = rest

    @pl.when(pl.program_id(0) == 0)
    def _():
        if has_h0:
            sre_o[...] = h0re_ref[...]
            sim_o[...] = h0im_ref[...]
        else:
            sre_o[...] = jnp.zeros_like(sre_o)
            sim_o[...] = jnp.zeros_like(sim_o)

    ar = are_ref[...]
    ai = aim_ref[...]
    dt = jnp.exp(ldt_ref[...])
    mag = jnp.exp(dt * ar)
    ab_re = mag * jnp.cos(dt * ai)
    ab_im = mag * jnp.sin(dt * ai)
    den = ar * ar + ai * ai
    nr = ab_re - 1.0
    f_re = (nr * ar + ab_im * ai) / den
    f_im = (ab_im * ar - nr * ai) / den

    u = u_ref[...].reshape(tt * nb, S5_CH)
    bu_re = _dot(u, bre_ref[...])
    bu_im = _dot(u, bim_ref[...])
    xre[...] = f_re * bu_re - f_im * bu_im
    xim[...] = f_re * bu_im + f_im * bu_re

    a_re = jnp.broadcast_to(ab_re, (nb, S5_N))
    a_im = jnp.broadcast_to(ab_im, (nb, S5_N))

    def step(ti, carry):
        s_re, s_im = carry
        rows = pl.ds(pl.multiple_of(ti * nb, nb), nb)
        n_re = a_re * s_re - a_im * s_im + xre[rows, :]
        n_im = a_re * s_im + a_im * s_re + xim[rows, :]
        xre[rows, :] = n_re
        xim[rows, :] = n_im
        return n_re, n_im

    s_re, s_im = lax.fori_loop(0, tt, step, (sre_o[...], sim_o[...]), unroll=2)
    sre_o[...] = s_re
    sim_o[...] = s_im

    y = _dot(xre[...], cre_ref[...]) - _dot(xim[...], cim_ref[...]) + d_ref[...] * u
    g = _dot(y, wg_ref[...]) + bg_ref[...]
    out = g[:, 0:S5_CH] * (1.0 / (1.0 + jnp.exp(-g[:, S5_CH:2 * S5_CH])))
    o_ref[...] = out.reshape(tt, nb, S5_CH)


def _s5(u, h0, w, tt):
    nb, t, _ = u.shape
    u = jnp.swapaxes(u, 0, 1)
    const = lambda shape: pl.BlockSpec(shape, lambda i: tuple(0 for _ in shape))
    in_specs = [pl.BlockSpec((tt, nb, S5_CH), lambda i: (i, 0, 0)), const((1, S5_N)), const((1, S5_N)),
                const((1, S5_N)), const((S5_CH, S5_N)), const((S5_CH, S5_N)), const((S5_N, S5_CH)),
                const((S5_N, S5_CH)), const((1, S5_CH)), const((S5_CH, 2 * S5_CH)), const((1, 2 * S5_CH))]
    args = [u, w["s5_a_re"], w["s5_a_im"], w["s5_log_dt"], w["s5_bre"], w["s5_bim"], w["s5_cre"], w["s5_cim"],
            w["s5_d"], w["s5_w_glu"], w["s5_b_glu"]]
    if h0 is not None:
        in_specs += [const((nb, S5_N)), const((nb, S5_N))]
        args += [h0[0], h0[1]]
    out, s_re, s_im = pl.pallas_call(
        functools.partial(_s5_body, h0 is not None, nb, tt),
        grid=(t // tt,),
        in_specs=in_specs,
        out_specs=[pl.BlockSpec((tt, nb, S5_CH), lambda i: (i, 0, 0)), const((nb, S5_N)), const((nb, S5_N))],
        out_shape=[jax.ShapeDtypeStruct((t, nb, S5_CH), F32), jax.ShapeDtypeStruct((nb, S5_N), F32),
                   jax.ShapeDtypeStruct((nb, S5_N), F32)],
        scratch_shapes=[pltpu.VMEM((tt * nb, S5_N), F32), pltpu.VMEM((tt * nb, S5_N), F32)],
        compiler_params=_params(("arbitrary",)),
        name="s5_scan",
    )(*args)
    return jnp.swapaxes(out, 0, 1).astype(BF), s_re, s_im


def _out_body(a_ref, b_ref, c_ref, x_ref, g1_ref, sc_ref, sh_ref, g2_ref, gn_ref, wa_ref, wb_ref, wc_ref, w1_ref,
              w2_ref, o_ref):
    mix = _dot(a_ref[0], wa_ref[...]) + _dot(b_ref[0], wb_ref[...]) + _dot(c_ref[0], wc_ref[...])
    x1 = x_ref[0] + g1_ref[0] * mix
    h2 = (x1 * _rs(x1, D_MODEL) * gn_ref[...] * (1.0 + sc_ref[0]) + sh_ref[0]).astype(BF)
    ff = jnp.zeros_like(x1)
    for c in range(D_FF // D_MODEL):
        hid = jnp.maximum(_dot(h2, w1_ref[:, c * D_MODEL:(c + 1) * D_MODEL]), 0.0)
        ff = ff + _dot(hid * hid, w2_ref[c * D_MODEL:(c + 1) * D_MODEL, :])
    o_ref[0] = x1 + g2_ref[0] * ff


def _out(a, b, c, x, g1, sc, sh, g2, w, tm):
    g, r, _ = x.shape
    per_token = sc.shape[1] != 1
    mod_spec = (pl.BlockSpec((1, tm, D_MODEL), lambda bi, i: (bi, i, 0)) if per_token
                else pl.BlockSpec((1, 1, D_MODEL), lambda bi, i: (bi, 0, 0)))
    tile = lambda n: pl.BlockSpec((1, tm, n), lambda bi, i: (bi, i, 0))
    const = lambda shape: pl.BlockSpec(shape, lambda bi, i: tuple(0 for _ in shape), pipeline_mode=pl.Buffered(1))
    return pl.pallas_call(
        _out_body,
        grid=(g, r // tm),
        in_specs=[tile(384), tile(384), tile(S5_CH), tile(D_MODEL), mod_spec, mod_spec, mod_spec, mod_spec,
                  const((1, D_MODEL)), const((384, D_MODEL)), const((384, D_MODEL)), const((S5_CH, D_MODEL)),
                  const((D_MODEL, D_FF)), const((D_FF, D_MODEL))],
        out_specs=tile(D_MODEL),
        out_shape=jax.ShapeDtypeStruct((g, r, D_MODEL), F32),
        compiler_params=_params(("arbitrary", "arbitrary")),
        name="out_mlp",
    )(a, b, c, x, g1, sc, sh, g2, w["norm2_g"], w["w_out_a"], w["w_out_b"], w["w_out_c"], w["ff_w1"], w["ff_w2"])


def _in_col_index():
    offs = np.cumsum((0,) + IN_SIZES)
    idx = np.full((IN_COLS_PADDED,), -1, np.int64)
    starts = (SEG_CQ, SEG_CKV, SEG_KR, SEG_BQ, SEG_BK, SEG_BV, SEG_IQ, SEG_IK, SEG_IW, SEG_U)
    for s, o, n in zip(starts, offs[:-1], IN_SIZES):
        idx[s:s + n] = o + np.arange(n)
    return idx


def _take_cols(wm, idx):
    return jnp.where(jnp.asarray(idx >= 0), jnp.take(wm, jnp.asarray(np.maximum(idx, 0)), axis=-1), 0.0)


def _slot_pad(v, lo, n):
    return jnp.pad(v, ((0, 0), (lo, LANE - lo - n)))


def _prep_weights(p):
    f = {}
    f["w_in"] = _take_cols(p["w_in"], _in_col_index()).astype(BF)
    qidx = np.full((H_A * LANE,), -1, np.int64)
    kidx = np.full((H_A * LANE,), -1, np.int64)
    vidx = np.zeros((H_A * V_D,), np.int64)
    for hh in range(H_A):
        qidx[hh * LANE:hh * LANE + QK_D] = hh * QK_D + np.arange(QK_D)
        kidx[hh * LANE:hh * LANE + NOPE_D] = hh * (NOPE_D + V_D) + np.arange(NOPE_D)
        vidx[hh * V_D:(hh + 1) * V_D] = hh * (NOPE_D + V_D) + NOPE_D + np.arange(V_D)
    f["mla_wuq"] = _take_cols(p["mla_wuq"], qidx).astype(BF)
    f["mla_wk"] = _take_cols(p["mla_wukv"], kidx).astype(BF)
    f["mla_wv"] = _take_cols(p["mla_wukv"], vidx).astype(BF)
    f["mla_gq_slot"] = jnp.concatenate([p["mla_gqn"], p["mla_gqr"], jnp.zeros((DEPTH, LANE - QK_D), F32)], axis=-1)
    f["mla_gkn_slot"] = _slot_pad(p["mla_gkn"], 0, NOPE_D)
    f["mla_gkr"] = _slot_pad(p["mla_gkr"], 0, ROPE_D)
    f["dsa_gq"] = jnp.concatenate([p["dsa_gq"], p["dsa_gq"]], axis=-1)
    f["dsa_gk"] = jnp.concatenate([p["dsa_gk"], p["dsa_gk"]], axis=-1)
    for name in ("norm1_g", "norm2_g", "mla_gq", "mla_gkv", "s5_d", "s5_b_glu"):
        f[name] = p[name]
    f["w_out_a"] = p["w_out"][:, 0:384].astype(BF)
    f["w_out_b"] = p["w_out"][:, 384:768].astype(BF)
    f["w_out_c"] = p["w_out"][:, 768:1024].astype(BF)
    f["ff_w1"] = p["ff_w1"].astype(BF)
    f["ff_w2"] = p["ff_w2"].astype(BF)
    f["s5_w_glu"] = p["s5_w_glu"].astype(BF)
    eye = jnp.eye(S5_G, dtype=F32)
    f["s5_bre"] = jnp.einsum("lgpc,gh->lgchp", p["s5_b_re"], eye).reshape(DEPTH, S5_CH, S5_N).astype(BF)
    f["s5_bim"] = jnp.einsum("lgpc,gh->lgchp", p["s5_b_im"], eye).reshape(DEPTH, S5_CH, S5_N).astype(BF)
    f["s5_cre"] = jnp.einsum("lgcp,gh->lgphc", p["s5_c_re"], eye).reshape(DEPTH, S5_N, S5_CH).astype(BF)
    f["s5_cim"] = jnp.einsum("lgcp,gh->lgphc", p["s5_c_im"], eye).reshape(DEPTH, S5_N, S5_CH).astype(BF)
    f["s5_a_re"] = p["s5_a_re"].reshape(DEPTH, S5_N)
    f["s5_a_im"] = p["s5_a_im"].reshape(DEPTH, S5_N)
    f["s5_log_dt"] = jnp.repeat(p["s5_log_dt"], S5_P, axis=-1)
    return f


def _layer_weights(f, l):
    w = {k: (v[l] if v.ndim == 3 else v[l][None, :]) for k, v in f.items()}
    place = np.zeros((ROPE_D, LANE), np.float32)
    place[np.arange(ROPE_D), NOPE_D + np.arange(ROPE_D)] = 1.0
    w["kr_place"] = jnp.asarray(place)
    tile = np.zeros((D_IDX, H_IDX * D_IDX), np.float32)
    for hh in range(H_IDX):
        tile[np.arange(D_IDX), hh * D_IDX + np.arange(D_IDX)] = 1.0
    w["idx_tile"] = jnp.asarray(tile, dtype=BF)
    return w


def _rope_tables(pos, lo):
    half = ROPE_D // 2
    inv = ROPE_BASE ** (-jnp.arange(half, dtype=F32) / half)
    ang = pos.astype(F32)[:, None] * inv[None, :]
    cos, sin = jnp.cos(ang), jnp.sin(ang)
    n = pos.shape[0]
    z = lambda w_: jnp.zeros((n, w_), F32)
    tail = LANE - lo - ROPE_D
    c = jnp.concatenate([jnp.ones((n, lo), F32), cos, cos, z(tail)], axis=-1)
    s1 = jnp.concatenate([z(lo), -sin, z(half), z(tail)], axis=-1)
    s2 = jnp.concatenate([z(lo), z(half), sin, z(tail)], axis=-1)
    return jnp.stack([c, s1, s2])


def _layer(x, mods, past, w, pos, per_token_mod, tiles):
    b, t, _ = x.shape
    tm, tq_mla, tq_dsa, tt = tiles
    sh1, sc1, g1, sh2, sc2, g2 = mods
    tabq = _rope_tables(pos, NOPE_D)
    tabk = _rope_tables(pos, 0)
    if per_token_mod:
        xf = x.reshape(1, b * t, D_MODEL)
        exp = lambda m: jnp.broadcast_to(m, (b, t, D_MODEL)).reshape(1, b * t, D_MODEL)
        sh1, sc1, g1, sh2, sc2, g2 = (exp(m) for m in mods)
        tabq = jnp.tile(tabq, (1, b, 1))
        tabk = jnp.tile(tabk, (1, b, 1))
    else:
        xf = x
    outs = _proj(xf, sc1, sh1, tabq, tabk, w, tm)
    q, ckv, kr, dq, dk, dkb, dv, dvb, iq, ik, ikt, iw, u = (o.reshape(b, t, o.shape[-1]) for o in outs)

    kvt = min(512, b * t)
    kn, vn = _kvup(ckv.reshape(b * t, KV_LORA), kr.reshape(b * t, ROPE_D), w, kvt)
    kn = kn.reshape(b, t, H_A * LANE)
    vn = vn.reshape(b, t, H_A * V_D)
    if past is None:
        a_out = _mla(q, kn, vn, None, None, tq_mla)
        b_out = _dsa(dq, iq, iw, dkb, dvb, ikt, None, None, None, w, tq_dsa)
        h0 = None
    else:
        p_ckv, p_kr, p_k, p_v, p_ik, p_sre, p_sim = past
        plen = p_ckv.shape[1]
        kp, vp = _kvup(p_ckv.reshape(b * plen, KV_LORA), p_kr.reshape(b * plen, ROPE_D), w, min(512, b * plen))
        a_out = _mla(q, kn, vn, kp.reshape(b, plen, H_A * LANE), vp.reshape(b, plen, H_A * V_D), tq_mla)
        b_out = _dsa(dq, iq, iw, dkb, dvb, ik, p_k.reshape(b, plen, 384), p_v.reshape(b, plen, 384), p_ik, w, tq_dsa)
        h0 = (p_sre.reshape(b, S5_N), p_sim.reshape(b, S5_N))
    c_out, s_re, s_im = _s5(u, h0, w, tt)

    shp = (1, b * t, -1) if per_token_mod else (b, t, -1)
    y = _out(a_out.reshape(shp), b_out.reshape(shp), c_out.reshape(shp), xf, g1, sc2, sh2, g2, w, tm)
    state = (ckv, kr, dk.reshape(b, t, H_B, HD_B), dv.reshape(b, t, H_B, HD_B), ik,
             s_re.reshape(b, S5_G, S5_P), s_im.reshape(b, S5_G, S5_P))
    return y.reshape(b, t, D_MODEL), state


def kernel(x_prompt, x_sample, c_prompt, c_sample, cache_mla_ckv, cache_mla_krope, cache_dsa_k, cache_dsa_v, cache_dsa_idxk, state_s5_re, state_s5_im, ada_w, ada_b, norm1_g, norm2_g, w_in, w_out, mla_gq, mla_wuq, mla_gkv, mla_wukv, mla_gqn, mla_gqr, mla_gkn, mla_gkr, dsa_gq, dsa_gk, s5_a_re, s5_a_im, s5_b_re, s5_b_im, s5_c_re, s5_c_im, s5_d, s5_log_dt, s5_w_glu, s5_b_glu, ff_w1, ff_w2):
    params = dict(norm1_g=norm1_g, norm2_g=norm2_g, w_in=w_in, w_out=w_out, mla_gq=mla_gq, mla_wuq=mla_wuq,
                  mla_gkv=mla_gkv, mla_wukv=mla_wukv, mla_gqn=mla_gqn, mla_gqr=mla_gqr, mla_gkn=mla_gkn,
                  mla_gkr=mla_gkr, dsa_gq=dsa_gq, dsa_gk=dsa_gk, s5_a_re=s5_a_re, s5_a_im=s5_a_im, s5_b_re=s5_b_re,
                  s5_b_im=s5_b_im, s5_c_re=s5_c_re, s5_c_im=s5_c_im, s5_d=s5_d, s5_log_dt=s5_log_dt,
                  s5_w_glu=s5_w_glu, s5_b_glu=s5_b_glu, ff_w1=ff_w1, ff_w2=ff_w2)
    f = _prep_weights(params)
    nbp, tp, _ = x_prompt.shape
    nbs, ts, _ = x_sample.shape
    past_len = cache_mla_ckv.shape[2]
    mod = _ada(jnp.concatenate([c_prompt, c_sample], axis=0), ada_w, ada_b)
    pos_p = jnp.arange(tp, dtype=jnp.int32)
    pos_s = past_len + jnp.arange(ts, dtype=jnp.int32)
    xp, xs = x_prompt, x_sample
    new_p = [[] for _ in range(7)]
    new_s = [[] for _ in range(7)]
    for l in range(DEPTH):
        w = _layer_weights(f, l)
        mods = [mod[l][:, None, i * D_MODEL:(i + 1) * D_MODEL] for i in range(6)]
        xp, st_p = _layer(xp, [m[:nbp] for m in mods], None, w, pos_p, False,
                          (min(512, tp), min(256, tp), min(128, tp), min(128, tp)))
        past = (cache_mla_ckv[l], cache_mla_krope[l], cache_dsa_k[l], cache_dsa_v[l], cache_dsa_idxk[l],
                state_s5_re[l], state_s5_im[l])
        xs, st_s = _layer(xs, [m[nbp:] for m in mods], past, w, pos_s, True, (nbs * ts, ts, ts, ts))
        for i in range(7):
            new_p[i].append(st_p[i])
            new_s[i].append(st_s[i])
    outs = [xp, xs]
    for i in range(7):
        outs.append(jnp.stack(new_p[i]))
        outs.append(jnp.stack(new_s[i]))
    return tuple(outs)
```

```python
import functools
import math

import numpy as np
import jax
import jax.numpy as jnp
from jax import lax
from jax.experimental import pallas as pl
from jax.experimental.pallas import tpu as pltpu

D_MODEL = 1024
DEPTH = 4
CHUNK_SHIFT = 6
EPS = 1e-6
D_FF = 4 * D_MODEL
H_A = 6
NOPE_D = 64
ROPE_D = 32
V_D = 64
QK_D = NOPE_D + ROPE_D
Q_LORA = 256
KV_LORA = 128
ROPE_BASE = 10000.0
H_B = 6
HD_B = 64
H_IDX = 8
D_IDX = 32
TOPK_MAX = 256
S5_CH = D_MODEL // 4
S5_GROUP = 16
S5_G = S5_CH // S5_GROUP
S5_P = 64
S5_N = S5_G * S5_P
IN_SIZES = (Q_LORA, KV_LORA, ROPE_D, H_B * HD_B, H_B * HD_B, H_B * HD_B, H_IDX * D_IDX, D_IDX, H_IDX, S5_CH)

LANE = 128
BF = jnp.bfloat16
F32 = jnp.float32
NEG = -1e30
INT_MIN = -2 ** 31
VMEM_LIMIT = 56 * 1024 * 1024
CAUSAL_SEG = 256
KEY_BLOCK = 128
KEY_OF_NEG_INF = -2139095041

SEG_CQ, SEG_CKV, SEG_KR, SEG_BQ, SEG_BK, SEG_BV, SEG_IQ, SEG_IK, SEG_IW, SEG_U = (
    0, 256, 384, 512, 896, 1280, 1664, 1920, 2048, 2176)
IN_COLS_PADDED = 2432


def _dot(a, b):
    return jnp.dot(a.astype(BF), b.astype(BF), preferred_element_type=F32)


def _dot_nt(a, b):
    return lax.dot_general(a.astype(BF), b.astype(BF), (((1,), (1,)), ((), ())), preferred_element_type=F32)


def _rs(x, n):
    return lax.rsqrt(jnp.sum(x * x, axis=-1, keepdims=True) * (1.0 / n) + EPS)


def _half_rs(x, lo, n_lo, n_hi):
    sq = x * x
    rs_lo = lax.rsqrt(jnp.sum(jnp.where(lo, sq, 0.0), axis=-1, keepdims=True) * (1.0 / n_lo) + EPS)
    rs_hi = lax.rsqrt(jnp.sum(jnp.where(lo, 0.0, sq), axis=-1, keepdims=True) * (1.0 / n_hi) + EPS)
    return jnp.where(lo, rs_lo, rs_hi)


def _rope(x, tab_ref):
    return x * tab_ref[0] + pltpu.roll(x, LANE - ROPE_D // 2, 1) * tab_ref[1] + pltpu.roll(x, ROPE_D // 2, 1) * tab_ref[2]


def _params(sem):
    return pltpu.CompilerParams(dimension_semantics=sem, vmem_limit_bytes=VMEM_LIMIT)


def _ada_body(c_ref, w_ref, b_ref, o_ref):
    c = c_ref[...]
    s = c * (1.0 / (1.0 + jnp.exp(-c)))
    o_ref[0] = _dot(s, w_ref[0]) + b_ref[0]


def _ada(c_all, ada_w, ada_b):
    n = c_all.shape[0]
    tn = 1536
    return pl.pallas_call(
        _ada_body,
        grid=(DEPTH, 6 * D_MODEL // tn),
        in_specs=[pl.BlockSpec((n, D_MODEL), lambda l, j: (0, 0)),
                  pl.BlockSpec((1, D_MODEL, tn), lambda l, j: (l, 0, j)),
                  pl.BlockSpec((1, 1, tn), lambda l, j: (l, 0, j))],
        out_specs=pl.BlockSpec((1, n, tn), lambda l, j: (l, 0, j)),
        out_shape=jax.ShapeDtypeStruct((DEPTH, n, 6 * D_MODEL), F32),
        compiler_params=_params(("arbitrary", "arbitrary")),
        name="ada_mod",
    )(c_all, ada_w, ada_b.reshape(DEPTH, 1, 6 * D_MODEL))


def _proj_body(kv_transposed, x_ref, sc_ref, sh_ref, gn_ref, win_ref, gq_ref, wuq_ref, gqs_ref, tq_ref, gkv_ref, gkr_ref,
               tk_ref, gdq_ref, gdk_ref, tile_ref, *rest):
    q_o, ckv_o, kr_o, dq_o, dk_o, dkb_o, dv_o, dvb_o, iq_o, ik_o, ikt_o, iw_o, u_o = rest[-13:]
    x = x_ref[0]
    tm = x.shape[0]
    xn = x * _rs(x, D_MODEL)
    h = xn * gn_ref[...] * (1.0 + sc_ref[0]) + sh_ref[0]
    z = _dot(h, win_ref[...])
    lo = lax.broadcasted_iota(jnp.int32, (tm, LANE), 1) < (LANE // 2)

    cq = z[:, SEG_CQ:SEG_CQ + Q_LORA]
    cqn = cq * _rs(cq, Q_LORA) * gq_ref[...]
    q = _dot(cqn, wuq_ref[...])
    for hh in range(H_A):
        qs = q[:, hh * LANE:(hh + 1) * LANE]
        qn = qs * _half_rs(qs, lo, NOPE_D, ROPE_D) * gqs_ref[...]
        q_o[0, :, hh * LANE:(hh + 1) * LANE] = (_rope(qn, tq_ref) * (QK_D ** -0.5)).astype(BF)

    ckv = z[:, SEG_CKV:SEG_CKV + KV_LORA]
    ckv_o[0] = ckv * _rs(ckv, KV_LORA) * gkv_ref[...]
    krs = z[:, SEG_KR:SEG_KR + LANE]
    krn = krs * _rs(krs, ROPE_D) * gkr_ref[...]
    kr_o[0] = _rope(krn, tk_ref)[:, 0:ROPE_D]

    for j in range(H_B // 2):
        bq = z[:, SEG_BQ + j * LANE:SEG_BQ + (j + 1) * LANE]
        dq_o[0, :, j * LANE:(j + 1) * LANE] = (
            bq * _half_rs(bq, lo, HD_B, HD_B) * gdq_ref[...] * (HD_B ** -0.5)).astype(BF)
        bk = z[:, SEG_BK + j * LANE:SEG_BK + (j + 1) * LANE]
        kn = bk * _half_rs(bk, lo, HD_B, HD_B) * gdk_ref[...]
        bv = z[:, SEG_BV + j * LANE:SEG_BV + (j + 1) * LANE]
        dkb_o[0, :, j * LANE:(j + 1) * LANE] = kn.astype(BF)
        if kv_transposed:
            bvt = jnp.transpose(bv)
            dk_o[0, 0, j * LANE:(j + 1) * LANE, :] = jnp.transpose(kn)
            dv_o[0, 0, j * LANE:(j + 1) * LANE, :] = bvt
            dvb_o[0, j * LANE:(j + 1) * LANE, :] = bvt.astype(BF)
        else:
            dk_o[0, :, j * LANE:(j + 1) * LANE] = kn
            dv_o[0, :, j * LANE:(j + 1) * LANE] = bv
            dvb_o[0, :, j * LANE:(j + 1) * LANE] = bv.astype(BF)
    iq_o[0] = (z[:, SEG_IQ:SEG_IQ + H_IDX * D_IDX] * (D_IDX ** -0.5)).astype(BF)
    ik = z[:, SEG_IK:SEG_IK + D_IDX]
    ik_o[0] = ik
    ikt_o[0] = _dot(ik, tile_ref[...]).astype(BF)
    iw_o[0] = z[:, SEG_IW:SEG_IW + LANE] * (H_IDX ** -0.5)
    u_o[0] = z[:, SEG_U:SEG_U + S5_CH]


def _proj(x, sc, sh, tabq, tabk, w, tm, kv_stack=None, layer=0):
    g, r, _ = x.shape
    per_token = sc.shape[1] != 1
    mod_spec = (pl.BlockSpec((1, tm, D_MODEL), lambda b, i: (b, i, 0)) if per_token
                else pl.BlockSpec((1, 1, D_MODEL), lambda b, i: (b, 0, 0)))
    const = lambda shape: pl.BlockSpec(shape, lambda b, i: tuple(0 for _ in shape))
    tab_spec = pl.BlockSpec((3, tm, LANE), lambda b, i: (0, i, 0))
    rows = lambda n, dt: (pl.BlockSpec((1, tm, n), lambda b, i: (b, i, 0)), jax.ShapeDtypeStruct((g, r, n), dt))
    kv = 384
    if kv_stack is None:
        dk, dv, dvb = rows(kv, F32), rows(kv, F32), rows(kv, BF)
    else:
        stack = (pl.BlockSpec((1, 1, kv, tm), lambda b, i: (layer, b, 0, i)),
                 jax.ShapeDtypeStruct((DEPTH, g, kv, r), F32))
        dk, dv = stack, stack
        dvb = (pl.BlockSpec((1, kv, tm), lambda b, i: (b, 0, i)), jax.ShapeDtypeStruct((g, kv, r), BF))
    outs = [rows(H_A * LANE, BF), rows(KV_LORA, F32), rows(ROPE_D, F32), rows(kv, BF), dk, rows(kv, BF), dv, dvb,
            rows(H_IDX * D_IDX, BF), rows(D_IDX, F32), rows(H_IDX * D_IDX, BF), rows(LANE, F32), rows(S5_CH, F32)]
    in_specs = [pl.BlockSpec((1, tm, D_MODEL), lambda b, i: (b, i, 0)), mod_spec, mod_spec,
                const((1, D_MODEL)), const((D_MODEL, IN_COLS_PADDED)), const((1, Q_LORA)),
                const((Q_LORA, H_A * LANE)), const((1, LANE)), tab_spec, const((1, KV_LORA)), const((1, LANE)),
                tab_spec, const((1, LANE)), const((1, LANE)), const((D_IDX, H_IDX * D_IDX))]
    args = [x, sc, sh, w["norm1_g"], w["w_in"], w["mla_gq"], w["mla_wuq"], w["mla_gq_slot"], tabq, w["mla_gkv"],
            w["mla_gkr"], tabk, w["dsa_gq"], w["dsa_gk"], w["idx_tile"]]
    aliases = {}
    if kv_stack is not None and kv_stack != "new":
        aliases = {len(args): 4, len(args) + 1: 6}
        in_specs += [pl.BlockSpec(memory_space=pl.ANY), pl.BlockSpec(memory_space=pl.ANY)]
        args += list(kv_stack)
    return pl.pallas_call(
        functools.partial(_proj_body, kv_stack is not None),
        grid=(g, r // tm),
        in_specs=in_specs,
        out_specs=[o[0] for o in outs],
        out_shape=[o[1] for o in outs],
        input_output_aliases=aliases,
        compiler_params=_params(("arbitrary", "arbitrary")),
        name="in_proj",
    )(*args)


def _kvup_body(ckv_ref, kr_ref, wk_ref, wv_ref, gkn_ref, place_ref, k_o, v_o):
    ckv = ckv_ref[0]
    krp = jnp.dot(kr_ref[...], place_ref[...], precision=lax.Precision.HIGHEST, preferred_element_type=F32)
    kn = _dot(ckv, wk_ref[...])
    for hh in range(H_A):
        ks = kn[:, hh * LANE:(hh + 1) * LANE]
        k_o[:, hh * LANE:(hh + 1) * LANE] = (ks * _rs(ks, NOPE_D) * gkn_ref[...] + krp).astype(BF)
    v_o[...] = _dot(ckv, wv_ref[...]).astype(BF)


def _kvup(ckv, kr, w, tm, layer=0):
    n = ckv.shape[1]
    const = lambda shape: pl.BlockSpec(shape, lambda i: tuple(0 for _ in shape))
    return pl.pallas_call(
        _kvup_body,
        grid=(n // tm,),
        in_specs=[pl.BlockSpec((1, tm, KV_LORA), lambda i: (layer, i, 0)), pl.BlockSpec((tm, ROPE_D), lambda i: (i, 0)),
                  const((KV_LORA, H_A * LANE)), const((KV_LORA, H_A * V_D)), const((1, LANE)),
                  const((ROPE_D, LANE))],
        out_specs=[pl.BlockSpec((tm, H_A * LANE), lambda i: (i, 0)), pl.BlockSpec((tm, H_A * V_D), lambda i: (i, 0))],
        out_shape=[jax.ShapeDtypeStruct((n, H_A * LANE), BF), jax.ShapeDtypeStruct((n, H_A * V_D), BF)],
        compiler_params=_params(("arbitrary",)),
        name="mla_kv_up",
    )(ckv, kr, w["mla_wk"], w["mla_wv"], w["mla_gkn_slot"], w["kr_place"])


def _visible(tq, sp, q0, s_valid):
    qpos = q0 + lax.broadcasted_iota(jnp.int32, (tq, sp), 0)
    kpos = lax.broadcasted_iota(jnp.int32, (tq, sp), 1)
    vis = lax.shift_right_logical(kpos, CHUNK_SHIFT) <= lax.shift_right_logical(qpos, CHUNK_SHIFT)
    if s_valid < sp:
        vis = jnp.logical_and(vis, kpos < s_valid)
    return vis, qpos, kpos


def _gather_keys(past_ref, new_ref, all_ref, p, t):
    sp = all_ref.shape[0]
    all_ref[0:p, :] = past_ref[0].astype(all_ref.dtype)
    all_ref[p:p + t, :] = new_ref[0].astype(all_ref.dtype)
    if p + t < sp:
        all_ref[p + t:sp, :] = jnp.zeros((sp - p - t, all_ref.shape[1]), all_ref.dtype)
    return all_ref[...]


def _causal_variants(t, tq, compute):
    seg = min(CAUSAL_SEG, t)
    i = pl.program_id(1)
    for sg in range(t // seg):
        first, last = sg * seg // tq, (sg + 1) * seg // tq
        pl.when(jnp.logical_and(i >= first, i < last))(functools.partial(compute, (sg + 1) * seg))


def _mla_body(p, t, tq, q_ref, k_ref, v_ref, *rest):
    if p:
        kp_ref, vp_ref, o_ref, kall, vall = rest
        k_all = _gather_keys(kp_ref, k_ref, kall, p, t)
        v_all = _gather_keys(vp_ref, v_ref, vall, p, t)
        _mla_compute(p, t, tq, q_ref, lambda n: k_all, lambda n: v_all, o_ref, k_all.shape[0])
    else:
        (o_ref,) = rest
        _causal_variants(t, tq, functools.partial(
            _mla_compute, p, t, tq, q_ref, lambda n: k_ref[0, 0:n, :], lambda n: v_ref[0, 0:n, :], o_ref))


def _mla_compute(p, t, tq, q_ref, load_k, load_v, o_ref, svis):
    k = load_k(svis)
    v = load_v(svis)
    q = q_ref[0]
    vis, _, _ = _visible(tq, svis, p + pl.program_id(1) * tq, p + t)
    lo = lax.broadcasted_iota(jnp.int32, (tq, LANE), 1) < (LANE // 2)
    for j in range(H_A // 2):
        vpair = v[:, j * LANE:(j + 1) * LANE]
        outs = []
        for e in range(2):
            hh = 2 * j + e
            s = _dot_nt(q[:, hh * LANE:(hh + 1) * LANE], k[:, hh * LANE:(hh + 1) * LANE])
            s = jnp.where(vis, s, NEG)
            pr = jnp.exp(s - jnp.max(s, axis=-1, keepdims=True))
            outs.append(_dot(pr, vpair) / jnp.sum(pr, axis=-1, keepdims=True))
        o_ref[0, :, j * LANE:(j + 1) * LANE] = jnp.where(lo, outs[0], outs[1]).astype(BF)


def _mla(q, k, v, kp, vp, tq):
    b, t, _ = q.shape
    p = 0 if kp is None else kp.shape[1]
    sp = -(-(p + t) // LANE) * LANE
    batch = lambda n, w: pl.BlockSpec((1, n, w), lambda bi, i: (bi, 0, 0))
    in_specs = [pl.BlockSpec((1, tq, H_A * LANE), lambda bi, i: (bi, i, 0)), batch(t, H_A * LANE), batch(t, H_A * V_D)]
    args = [q, k, v]
    scratch = []
    if p:
        in_specs += [batch(p, H_A * LANE), batch(p, H_A * V_D)]
        args += [kp, vp]
        scratch = [pltpu.VMEM((sp, H_A * LANE), BF), pltpu.VMEM((sp, H_A * V_D), BF)]
    return pl.pallas_call(
        functools.partial(_mla_body, p, t, tq),
        grid=(b, t // tq),
        in_specs=in_specs,
        out_specs=pl.BlockSpec((1, tq, H_A * V_D), lambda bi, i: (bi, i, 0)),
        out_shape=jax.ShapeDtypeStruct((b, t, H_A * V_D), BF),
        scratch_shapes=scratch,
        compiler_params=_params(("arbitrary", "arbitrary")),
        name="mla_attn",
    )(*args)


def _key_to_f32(key):
    return lax.bitcast_convert_type(jnp.where(key >= 0, key, key ^ 0x7FFFFFFF), F32)


def _kth_largest(count_ge, shape, kk):
    t0 = jnp.where(count_ge(jnp.zeros(shape, F32)) >= kk, 0, INT_MIN).astype(jnp.int32)

    def bit_step(i, tk):
        cand = tk + lax.shift_left(jnp.int32(1), 30 - i)
        return jnp.where(count_ge(_key_to_f32(cand)) >= kk, cand, tk)

    tk = lax.fori_loop(0, 31, bit_step, t0)
    return _key_to_f32(jnp.maximum(tk, KEY_OF_NEG_INF))


def _dsa_t_body(t, tq, top, q_ref, iq_ref, iw_ref, k_ref, vt_ref, ikt_ref, o_ref, sc_ref, s_ref, p_ref):
    _causal_variants(t, tq, functools.partial(_dsa_t_compute, tq, top, q_ref, iq_ref, iw_ref, k_ref, vt_ref, ikt_ref,
                                              o_ref, sc_ref, s_ref, p_ref))


def _dsa_t_compute(tq, top, q_ref, iq_ref, iw_ref, k_ref, vt_ref, ikt_ref, o_ref, sc_ref, s_ref, p_ref, svis):
    q0 = pl.program_id(1) * tq
    nkb = svis // KEY_BLOCK
    sub = KEY_BLOCK // 8
    qpos = q0 + lax.broadcasted_iota(jnp.int32, (KEY_BLOCK, tq), 1)
    kpos0 = lax.broadcasted_iota(jnp.int32, (KEY_BLOCK, tq), 0)

    def visible(kb):
        if (kb + 1) * KEY_BLOCK <= svis - CAUSAL_SEG:
            return None
        return lax.shift_right_logical(kpos0 + kb * KEY_BLOCK, CHUNK_SHIFT) <= lax.shift_right_logical(qpos, CHUNK_SHIFT)

    iq = iq_ref[0]
    iw_t = jnp.transpose(iw_ref[0])
    head_of_lane = lax.shift_right_logical(lax.broadcasted_iota(jnp.int32, (tq, H_IDX * D_IDX), 1), 5)
    qhs = [jnp.where(head_of_lane == hh, iq, jnp.zeros_like(iq)) for hh in range(H_IDX)]
    for kb in range(nkb):
        rows = slice(kb * KEY_BLOCK, (kb + 1) * KEY_BLOCK)
        ikt = ikt_ref[0, rows, :]
        acc = jnp.zeros((KEY_BLOCK, tq), F32)
        for hh in range(H_IDX):
            acc = acc + iw_t[hh:hh + 1, :] * jnp.maximum(_dot_nt(ikt, qhs[hh]), 0.0)
        vis = visible(kb)
        sc_ref[rows, :] = acc if vis is None else jnp.where(vis, acc, -jnp.inf)

    def count_ge(cand):
        x = sc_ref[0:svis, :].reshape(8, svis // 64, 8, tq)
        part = jnp.sum(jnp.where(x >= cand, 1.0, 0.0), axis=1)
        return jnp.sum(jnp.sum(part, axis=0), axis=0, keepdims=True)

    thr = _kth_largest(count_ge, (1, tq), float(top))

    q = q_ref[0]
    lo = lax.broadcasted_iota(jnp.int32, (tq, LANE), 1) < (LANE // 2)
    qms = []
    for hh in range(H_B):
        qpair = q[:, (hh // 2) * LANE:(hh // 2 + 1) * LANE]
        qms.append(jnp.where(lo if hh % 2 == 0 else jnp.logical_not(lo), qpair, jnp.zeros_like(qpair)))
    m_acc = [jnp.full((8, tq), NEG, F32) for _ in range(H_B)]
    for kb in range(nkb):
        rows = slice(kb * KEY_BLOCK, (kb + 1) * KEY_BLOCK)
        sel = sc_ref[rows, :] >= thr
        vis = visible(kb)
        if vis is not None:
            sel = jnp.logical_and(sel, vis)
        bias = jnp.where(sel, 0.0, NEG)
        dist = jnp.abs(qpos - (kpos0 + kb * KEY_BLOCK)).astype(F32)
        for hh in range(H_B):
            kpair = k_ref[0, rows, (hh // 2) * LANE:(hh // 2 + 1) * LANE]
            s = _dot_nt(kpair, qms[hh]) - (2.0 ** (-8.0 * (hh + 1) / H_B)) * dist + bias
            s_ref[hh, rows, :] = s
            m_acc[hh] = jnp.maximum(m_acc[hh], jnp.max(s.reshape(sub, 8, tq), axis=0))

    for j in range(H_B // 2):
        halves = []
        for e in range(2):
            hh = 2 * j + e
            m = jnp.max(m_acc[hh], axis=0, keepdims=True)
            for kb in range(nkb):
                rows = slice(kb * KEY_BLOCK, (kb + 1) * KEY_BLOCK)
                p_ref[rows, :] = jnp.exp(s_ref[hh, rows, :] - m).astype(BF)
            pw = p_ref[0:svis, :]
            out_t = jnp.dot(vt_ref[0, j * LANE:(j + 1) * LANE, 0:svis], pw, preferred_element_type=F32)
            out_t = out_t / jnp.dot(jnp.ones((8, svis), BF), pw, preferred_element_type=F32)[0:1, :]
            halves.append(out_t[e * (LANE // 2):(e + 1) * (LANE // 2), :])
        o_ref[0, :, j * LANE:(j + 1) * LANE] = jnp.transpose(jnp.concatenate(halves, axis=0)).astype(BF)


def _dsa_prompt(q, iq, iw, k, vt, ikt, tq):
    b, t, _ = q.shape
    top = min(TOPK_MAX, t // 4)
    tile = lambda wd: pl.BlockSpec((1, tq, wd), lambda bi, i: (bi, i, 0))
    batch = lambda n, wd: pl.BlockSpec((1, n, wd), lambda bi, i: (bi, 0, 0))
    return pl.pallas_call(
        functools.partial(_dsa_t_body, t, tq, top),
        grid=(b, t // tq),
        in_specs=[tile(384), tile(256), tile(LANE), batch(t, 384), batch(384, t), batch(t, 256)],
        out_specs=tile(384),
        out_shape=jax.ShapeDtypeStruct((b, t, 384), BF),
        scratch_shapes=[pltpu.VMEM((t, tq), F32), pltpu.VMEM((H_B, t, tq), F32), pltpu.VMEM((t, tq), BF)],
        compiler_params=_params(("arbitrary", "arbitrary")),
        name="dsa_attn",
    )(q, iq, iw, k, vt, ikt)


def _dsa_s_body(p, t, top, q_ref, iq_ref, iw_ref, k_ref, v_ref, ikt_ref, kpt_ref, vpt_ref, ikpt_ref, o_ref, sc_ref):
    sp = sc_ref.shape[1]
    pad_rows = lambda x: jnp.concatenate([x, jnp.zeros((LANE - t, x.shape[1]), x.dtype)], axis=0)
    lane = lax.broadcasted_iota(jnp.int32, (t, sp), 1)
    qrow = p + lax.broadcasted_iota(jnp.int32, (t, sp), 0)
    valid = jnp.logical_and(
        lane < p + t, lax.shift_right_logical(lane, CHUNK_SHIFT) <= lax.shift_right_logical(qrow, CHUNK_SHIFT))

    iq = iq_ref[0]
    iw = iw_ref[0]
    ikp = ikpt_ref[0, 0].astype(BF)
    ikp_tiled = jnp.concatenate([ikp] * H_IDX, axis=0)
    ikn = pad_rows(ikt_ref[0])
    head_of_lane = lax.shift_right_logical(lax.broadcasted_iota(jnp.int32, (t, H_IDX * D_IDX), 1), 5)
    sc_p = jnp.zeros((t, p), F32)
    sc_n = jnp.zeros((t, LANE), F32)
    for hh in range(H_IDX):
        qh = jnp.where(head_of_lane == hh, iq, jnp.zeros_like(iq))
        wh = iw[:, hh:hh + 1]
        sc_p = sc_p + wh * jnp.maximum(jnp.dot(qh, ikp_tiled, preferred_element_type=F32), 0.0)
        sc_n = sc_n + wh * jnp.maximum(_dot_nt(qh, ikn), 0.0)
    sc_ref[...] = jnp.where(valid, jnp.concatenate([sc_p, sc_n], axis=1), -jnp.inf)

    def count_ge(cand):
        return jnp.sum(jnp.where(sc_ref[...] >= cand, 1.0, 0.0), axis=1, keepdims=True)

    thr = _kth_largest(count_ge, (t, 1), float(top))
    bias = jnp.where(jnp.logical_and(sc_ref[...] >= thr, valid), 0.0, NEG)
    dist = jnp.abs(qrow - lane).astype(F32)

    q = q_ref[0]
    lo = lax.broadcasted_iota(jnp.int32, (t, LANE), 1) < (LANE // 2)
    for j in range(H_B // 2):
        cols = slice(j * LANE, (j + 1) * LANE)
        qpair = q[:, cols]
        kpt = kpt_ref[0, 0, cols, :].astype(BF)
        vpt = vpt_ref[0, 0, cols, :].astype(BF)
        kn = pad_rows(k_ref[0][:, cols])
        vn = pad_rows(v_ref[0][:, cols])
        outs = []
        for e in range(2):
            hh = 2 * j + e
            qh = jnp.where(lo if e == 0 else jnp.logical_not(lo), qpair, jnp.zeros_like(qpair))
            s = jnp.concatenate([jnp.dot(qh, kpt, preferred_element_type=F32), _dot_nt(qh, kn)], axis=1)
            s = s - (2.0 ** (-8.0 * (hh + 1) / H_B)) * dist + bias
            pr = jnp.exp(s - jnp.max(s, axis=-1, keepdims=True))
            prb = pr.astype(BF)
            o = _dot_nt(prb[:, 0:p], vpt) + jnp.dot(prb[:, p:sp], vn, preferred_element_type=F32)
            outs.append(o / jnp.sum(pr, axis=-1, keepdims=True))
        o_ref[0, :, cols] = jnp.where(lo, outs[0], outs[1]).astype(BF)


def _dsa_decode(q, iq, iw, k, v, ikt, kpt, vpt, ikpt, layer):
    b, t, _ = q.shape
    p = kpt.shape[-1]
    sp = p + LANE
    top = min(TOPK_MAX, (p + t) // 4)
    new = lambda wd: pl.BlockSpec((1, t, wd), lambda bi: (bi, 0, 0))
    cache = lambda n: pl.BlockSpec((1, 1, n, p), lambda bi: (layer, bi, 0, 0))
    return pl.pallas_call(
        functools.partial(_dsa_s_body, p, t, top),
        grid=(b,),
        in_specs=[new(384), new(256), new(LANE), new(384), new(384), new(256), cache(384), cache(384), cache(D_IDX)],
        out_specs=new(384),
        out_shape=jax.ShapeDtypeStruct((b, t, 384), BF),
        scratch_shapes=[pltpu.VMEM((t, sp), F32)],
        compiler_params=_params(("arbitrary",)),
        name="dsa_decode",
    )(q, iq, iw, k, v, ikt, kpt, vpt, ikpt)


def _s5_body(has_h0, nb, tt, u_ref, are_ref, aim_ref, ldt_ref, bre_ref, bim_ref, cre_ref, cim_ref, d_ref, wg_ref,
             bg_ref, *rest):
    if has_h0:
        h0re_ref, h0im_ref, o_ref, sre_o, sim_o, xre, xim = rest
    else:
        o_ref, sre_o, sim_o, xre, xim = rest

    @pl.when(pl.program_id(0) == 0)
    def _():
        if has_h0:
            sre_o[...] = h0re_ref[...]
            sim_o[...] = h0im_ref[...]
        else:
            sre_o[...] = jnp.zeros_like(sre_o)
            sim_o[...] = jnp.zeros_like(sim_o)

    ar = are_ref[...]
    ai = aim_ref[...]
    dt = jnp.exp(ldt_ref[...])
    mag = jnp.exp(dt * ar)
    ab_re = mag * jnp.cos(dt * ai)
    ab_im = mag * jnp.sin(dt * ai)
    den = ar * ar + ai * ai
    nr = ab_re - 1.0
    f_re = (nr * ar + ab_im * ai) / den
    f_im = (ab_im * ar - nr * ai) / den

    u = u_ref[...].reshape(tt * nb, S5_CH)
    bu_re = _dot(u, bre_ref[...])
    bu_im = _dot(u, bim_ref[...])
    xre[...] = f_re * bu_re - f_im * bu_im
    xim[...] = f_re * bu_im + f_im * bu_re

    a_re = jnp.broadcast_to(ab_re, (nb, S5_N))
    a_im = jnp.broadcast_to(ab_im, (nb, S5_N))

    def step(ti, carry):
        s_re, s_im = carry
        rows = pl.ds(pl.multiple_of(ti * nb, nb), nb)
        n_re = a_re * s_re - a_im * s_im + xre[rows, :]
        n_im = a_re * s_im + a_im * s_re + xim[rows, :]
        xre[rows, :] = n_re
        xim[rows, :] = n_im
        return n_re, n_im

    s_re, s_im = lax.fori_loop(0, tt, step, (sre_o[...], sim_o[...]), unroll=2)
    sre_o[...] = s_re
    sim_o[...] = s_im

    y = _dot(xre[...], cre_ref[...]) - _dot(xim[...], cim_ref[...]) + d_ref[...] * u
    g = _dot(y, wg_ref[...]) + bg_ref[...]
    out = g[:, 0:S5_CH] * (1.0 / (1.0 + jnp.exp(-g[:, S5_CH:2 * S5_CH])))
    o_ref[...] = out.reshape(tt, nb, S5_CH)


def _s5(u, h0, w, tt):
    nb, t, _ = u.shape
    u = jnp.swapaxes(u, 0, 1)
    const = lambda shape: pl.BlockSpec(shape, lambda i: tuple(0 for _ in shape))
    in_specs = [pl.BlockSpec((tt, nb, S5_CH), lambda i: (i, 0, 0)), const((1, S5_N)), const((1, S5_N)),
                const((1, S5_N)), const((S5_CH, S5_N)), const((S5_CH, S5_N)), const((S5_N, S5_CH)),
                const((S5_N, S5_CH)), const((1, S5_CH)), const((S5_CH, 2 * S5_CH)), const((1, 2 * S5_CH))]
    args = [u, w["s5_a_re"], w["s5_a_im"], w["s5_log_dt"], w["s5_bre"], w["s5_bim"], w["s5_cre"], w["s5_cim"],
            w["s5_d"], w["s5_w_glu"], w["s5_b_glu"]]
    if h0 is not None:
        in_specs += [const((nb, S5_N)), const((nb, S5_N))]
        args += [h0[0], h0[1]]
    out, s_re, s_im = pl.pallas_call(
        functools.partial(_s5_body, h0 is not None, nb, tt),
        grid=(t // tt,),
        in_specs=in_specs,
        out_specs=[pl.BlockSpec((tt, nb, S5_CH), lambda i: (i, 0, 0)), const((nb, S5_N)), const((nb, S5_N))],
        out_shape=[jax.ShapeDtypeStruct((t, nb, S5_CH), F32), jax.ShapeDtypeStruct((nb, S5_N), F32),
                   jax.ShapeDtypeStruct((nb, S5_N), F32)],
        scratch_shapes=[pltpu.VMEM((tt * nb, S5_N), F32), pltpu.VMEM((tt * nb, S5_N), F32)],
        compiler_params=_params(("arbitrary",)),
        name="s5_scan",
    )(*args)
    return jnp.swapaxes(out, 0, 1).astype(BF), s_re, s_im


def _out_body(a_ref, b_ref, c_ref, x_ref, g1_ref, sc_ref, sh_ref, g2_ref, gn_ref, wa_ref, wb_ref, wc_ref, w1_ref,
              w2_ref, o_ref):
    mix = _dot(a_ref[0], wa_ref[...]) + _dot(b_ref[0], wb_ref[...]) + _dot(c_ref[0], wc_ref[...])
    x1 = x_ref[0] + g1_ref[0] * mix
    h2 = (x1 * _rs(x1, D_MODEL) * gn_ref[...] * (1.0 + sc_ref[0]) + sh_ref[0]).astype(BF)
    ff = jnp.zeros_like(x1)
    for c in range(D_FF // D_MODEL):
        hid = jnp.maximum(_dot(h2, w1_ref[:, c * D_MODEL:(c + 1) * D_MODEL]), 0.0)
        ff = ff + _dot(hid * hid, w2_ref[c * D_MODEL:(c + 1) * D_MODEL, :])
    o_ref[0] = x1 + g2_ref[0] * ff


def _out(a, b, c, x, g1, sc, sh, g2, w, tm):
    g, r, _ = x.shape
    per_token = sc.shape[1] != 1
    mod_spec = (pl.BlockSpec((1, tm, D_MODEL), lambda bi, i: (bi, i, 0)) if per_token
                else pl.BlockSpec((1, 1, D_MODEL), lambda bi, i: (bi, 0, 0)))
    tile = lambda n: pl.BlockSpec((1, tm, n), lambda bi, i: (bi, i, 0))
    const = lambda shape: pl.BlockSpec(shape, lambda bi, i: tuple(0 for _ in shape), pipeline_mode=pl.Buffered(1))
    return pl.pallas_call(
        _out_body,
        grid=(g, r // tm),
        in_specs=[tile(384), tile(384), tile(S5_CH), tile(D_MODEL), mod_spec, mod_spec, mod_spec, mod_spec,
                  const((1, D_MODEL)), const((384, D_MODEL)), const((384, D_MODEL)), const((S5_CH, D_MODEL)),
                  const((D_MODEL, D_FF)), const((D_FF, D_MODEL))],
        out_specs=tile(D_MODEL),
        out_shape=jax.ShapeDtypeStruct((g, r, D_MODEL), F32),
        compiler_params=_params(("arbitrary", "arbitrary")),
        name="out_mlp",
    )(a, b, c, x, g1, sc, sh, g2, w["norm2_g"], w["w_out_a"], w["w_out_b"], w["w_out_c"], w["ff_w1"], w["ff_w2"])


def _slot_pad(v, lo, n):
    return jnp.pad(v, ((0, 0), (lo, LANE - lo - n)))


def _place_cols(wm, segments, total):
    parts, at = [], 0
    for dst, src, n in segments:
        if dst > at:
            parts.append(jnp.zeros(wm.shape[:-1] + (dst - at,), wm.dtype))
        parts.append(wm[..., src:src + n])
        at = dst + n
    if total > at:
        parts.append(jnp.zeros(wm.shape[:-1] + (total - at,), wm.dtype))
    return jnp.concatenate(parts, axis=-1)


def _prep_weights(p):
    f = {}
    offs = np.cumsum((0,) + IN_SIZES)
    starts = (SEG_CQ, SEG_CKV, SEG_KR, SEG_BQ, SEG_BK, SEG_BV, SEG_IQ, SEG_IK, SEG_IW, SEG_U)
    f["w_in"] = _place_cols(p["w_in"], [(s, int(o), n) for s, o, n in zip(starts, offs[:-1], IN_SIZES)],
                            IN_COLS_PADDED).astype(BF)
    kvw = NOPE_D + V_D
    f["mla_wuq"] = _place_cols(p["mla_wuq"], [(hh * LANE, hh * QK_D, QK_D) for hh in range(H_A)], H_A * LANE).astype(BF)
    f["mla_wk"] = _place_cols(p["mla_wukv"], [(hh * LANE, hh * kvw, NOPE_D) for hh in range(H_A)], H_A * LANE).astype(BF)
    f["mla_wv"] = _place_cols(p["mla_wukv"], [(hh * V_D, hh * kvw + NOPE_D, V_D) for hh in range(H_A)],
                              H_A * V_D).astype(BF)
    f["mla_gq_slot"] = jnp.concatenate([p["mla_gqn"], p["mla_gqr"], jnp.zeros((DEPTH, LANE - QK_D), F32)], axis=-1)
    f["mla_gkn_slot"] = _slot_pad(p["mla_gkn"], 0, NOPE_D)
    f["mla_gkr"] = _slot_pad(p["mla_gkr"], 0, ROPE_D)
    f["dsa_gq"] = jnp.concatenate([p["dsa_gq"], p["dsa_gq"]], axis=-1)
    f["dsa_gk"] = jnp.concatenate([p["dsa_gk"], p["dsa_gk"]], axis=-1)
    for name in ("norm1_g", "norm2_g", "mla_gq", "mla_gkv", "s5_d", "s5_b_glu"):
        f[name] = p[name]
    f["w_out_a"] = p["w_out"][:, 0:384].astype(BF)
    f["w_out_b"] = p["w_out"][:, 384:768].astype(BF)
    f["w_out_c"] = p["w_out"][:, 768:1024].astype(BF)
    f["ff_w1"] = p["ff_w1"].astype(BF)
    f["ff_w2"] = p["ff_w2"].astype(BF)
    f["s5_w_glu"] = p["s5_w_glu"].astype(BF)
    eye = jnp.eye(S5_G, dtype=F32)
    f["s5_bre"] = jnp.einsum("lgpc,gh->lgchp", p["s5_b_re"], eye).reshape(DEPTH, S5_CH, S5_N).astype(BF)
    f["s5_bim"] = jnp.einsum("lgpc,gh->lgchp", p["s5_b_im"], eye).reshape(DEPTH, S5_CH, S5_N).astype(BF)
    f["s5_cre"] = jnp.einsum("lgcp,gh->lgphc", p["s5_c_re"], eye).reshape(DEPTH, S5_N, S5_CH).astype(BF)
    f["s5_cim"] = jnp.einsum("lgcp,gh->lgphc", p["s5_c_im"], eye).reshape(DEPTH, S5_N, S5_CH).astype(BF)
    f["s5_a_re"] = p["s5_a_re"].reshape(DEPTH, S5_N)
    f["s5_a_im"] = p["s5_a_im"].reshape(DEPTH, S5_N)
    f["s5_log_dt"] = jnp.repeat(p["s5_log_dt"], S5_P, axis=-1)
    return f


def _layer_weights(f, l):
    w = {k: (v[l] if v.ndim == 3 else v[l][None, :]) for k, v in f.items()}
    place = np.zeros((ROPE_D, LANE), np.float32)
    place[np.arange(ROPE_D), NOPE_D + np.arange(ROPE_D)] = 1.0
    w["kr_place"] = jnp.asarray(place)
    tile = np.zeros((D_IDX, H_IDX * D_IDX), np.float32)
    for hh in range(H_IDX):
        tile[np.arange(D_IDX), hh * D_IDX + np.arange(D_IDX)] = 1.0
    w["idx_tile"] = jnp.asarray(tile, dtype=BF)
    return w


def _rope_tables(pos, lo):
    half = ROPE_D // 2
    inv = ROPE_BASE ** (-jnp.arange(half, dtype=F32) / half)
    ang = pos.astype(F32)[:, None] * inv[None, :]
    cos, sin = jnp.cos(ang), jnp.sin(ang)
    n = pos.shape[0]
    z = lambda w_: jnp.zeros((n, w_), F32)
    tail = LANE - lo - ROPE_D
    c = jnp.concatenate([jnp.ones((n, lo), F32), cos, cos, z(tail)], axis=-1)
    s1 = jnp.concatenate([z(lo), -sin, z(half), z(tail)], axis=-1)
    s2 = jnp.concatenate([z(lo), z(half), sin, z(tail)], axis=-1)
    return jnp.stack([c, s1, s2])


def _prompt_layer(x, mods, w, pos, tiles, layer, kv_stack):
    b, t, _ = x.shape
    tm, tq_mla, tq_dsa, tt = tiles
    sh1, sc1, g1, sh2, sc2, g2 = mods
    outs = _proj(x, sc1, sh1, _rope_tables(pos, NOPE_D), _rope_tables(pos, 0), w, tm, kv_stack, layer)
    q, ckv, kr, dq, dk_stack, dkb, dv_stack, dvtb, iq, ik, ikt, iw, u = outs
    kn, vn = _kvup(ckv.reshape(1, b * t, KV_LORA), kr.reshape(b * t, ROPE_D), w, min(512, b * t))
    a_out = _mla(q, kn.reshape(b, t, H_A * LANE), vn.reshape(b, t, H_A * V_D), None, None, tq_mla)
    b_out = _dsa_prompt(dq, iq, iw, dkb, dvtb, ikt, tq_dsa)
    c_out, s_re, s_im = _s5(u, None, w, tt)
    y = _out(a_out, b_out, c_out, x, g1, sc2, sh2, g2, w, tm)
    state = (ckv, kr, None, None, ik, s_re.reshape(b, S5_G, S5_P), s_im.reshape(b, S5_G, S5_P))
    return y, state, (dk_stack, dv_stack)


def _decode_layer(x, mods, caches, w, pos, layer):
    b, t, _ = x.shape
    n = b * t
    c_ckv, c_kr, c_kt, c_vt, c_ikt, c_sre, c_sim = caches
    plen = c_kt.shape[-1]
    xf = x.reshape(1, n, D_MODEL)
    sh1, sc1, g1, sh2, sc2, g2 = (jnp.broadcast_to(m, (b, t, D_MODEL)).reshape(1, n, D_MODEL) for m in mods)
    tabq = jnp.tile(_rope_tables(pos, NOPE_D), (1, b, 1))
    tabk = jnp.tile(_rope_tables(pos, 0), (1, b, 1))
    outs = _proj(xf, sc1, sh1, tabq, tabk, w, n)
    q, ckv, kr, dq, dk, dkb, dv, dvb, iq, ik, ikt, iw, u = (o.reshape(b, t, o.shape[-1]) for o in outs)
    kn, vn = _kvup(ckv.reshape(1, n, KV_LORA), kr.reshape(n, ROPE_D), w, n)
    kp, vp = _kvup(c_ckv, c_kr[layer].reshape(b * plen, ROPE_D), w, min(512, b * plen), layer)
    a_out = _mla(q, kn.reshape(b, t, H_A * LANE), vn.reshape(b, t, H_A * V_D), kp.reshape(b, plen, H_A * LANE),
                 vp.reshape(b, plen, H_A * V_D), t)
    b_out = _dsa_decode(dq, iq, iw, dkb, dvb, ikt, c_kt, c_vt, c_ikt, layer)
    c_out, s_re, s_im = _s5(u, (c_sre[layer], c_sim[layer]), w, t)
    y = _out(a_out.reshape(1, n, -1), b_out.reshape(1, n, -1), c_out.reshape(1, n, -1), xf, g1, sc2, sh2, g2, w, n)
    state = (ckv, kr, dk.reshape(b, t, H_B, HD_B), dv.reshape(b, t, H_B, HD_B), ik,
             s_re.reshape(b, S5_G, S5_P), s_im.reshape(b, S5_G, S5_P))
    return y.reshape(b, t, D_MODEL), state


def kernel(x_prompt, x_sample, c_prompt, c_sample, cache_mla_ckv, cache_mla_krope, cache_dsa_k, cache_dsa_v, cache_dsa_idxk, state_s5_re, state_s5_im, ada_w, ada_b, norm1_g, norm2_g, w_in, w_out, mla_gq, mla_wuq, mla_gkv, mla_wukv, mla_gqn, mla_gqr, mla_gkn, mla_gkr, dsa_gq, dsa_gk, s5_a_re, s5_a_im, s5_b_re, s5_b_im, s5_c_re, s5_c_im, s5_d, s5_log_dt, s5_w_glu, s5_b_glu, ff_w1, ff_w2):
    params = dict(norm1_g=norm1_g, norm2_g=norm2_g, w_in=w_in, w_out=w_out, mla_gq=mla_gq, mla_wuq=mla_wuq,
                  mla_gkv=mla_gkv, mla_wukv=mla_wukv, mla_gqn=mla_gqn, mla_gqr=mla_gqr, mla_gkn=mla_gkn,
                  mla_gkr=mla_gkr, dsa_gq=dsa_gq, dsa_gk=dsa_gk, s5_a_re=s5_a_re, s5_a_im=s5_a_im, s5_b_re=s5_b_re,
                  s5_b_im=s5_b_im, s5_c_re=s5_c_re, s5_c_im=s5_c_im, s5_d=s5_d, s5_log_dt=s5_log_dt,
                  s5_w_glu=s5_w_glu, s5_b_glu=s5_b_glu, ff_w1=ff_w1, ff_w2=ff_w2)
    f = _prep_weights(params)
    nbp, tp, _ = x_prompt.shape
    nbs, ts, _ = x_sample.shape
    past_len = cache_mla_ckv.shape[2]
    mod = _ada(jnp.concatenate([c_prompt, c_sample], axis=0), ada_w, ada_b)
    pos_p = jnp.arange(tp, dtype=jnp.int32)
    pos_s = past_len + jnp.arange(ts, dtype=jnp.int32)
    caches = (cache_mla_ckv.reshape(DEPTH, nbs * past_len, KV_LORA), cache_mla_krope,
              jnp.transpose(cache_dsa_k, (0, 1, 3, 4, 2)).reshape(DEPTH, nbs, H_B * HD_B, past_len),
              jnp.transpose(cache_dsa_v, (0, 1, 3, 4, 2)).reshape(DEPTH, nbs, H_B * HD_B, past_len),
              jnp.transpose(cache_dsa_idxk, (0, 1, 3, 2)),
              state_s5_re.reshape(DEPTH, nbs, S5_N), state_s5_im.reshape(DEPTH, nbs, S5_N))
    xp, xs = x_prompt, x_sample
    new_p = [[] for _ in range(7)]
    new_s = [[] for _ in range(7)]
    kv_stack = "new"
    for l in range(DEPTH):
        w = _layer_weights(f, l)
        mods = [mod[l][:, None, i * D_MODEL:(i + 1) * D_MODEL] for i in range(6)]
        xp, st_p, kv_stack = _prompt_layer(xp, [m[:nbp] for m in mods], w, pos_p,
                                           (min(512, tp), min(256, tp), min(128, tp), min(128, tp)), l, kv_stack)
        xs, st_s = _decode_layer(xs, [m[nbp:] for m in mods], caches, w, pos_s, l)
        for i in range(7):
            new_p[i].append(st_p[i])
            new_s[i].append(st_s[i])
    outs = [xp, xs]
    for i in range(7):
        if i in (2, 3):
            stacked = jnp.transpose(kv_stack[i - 2].reshape(DEPTH, nbp, H_B, HD_B, tp), (0, 1, 4, 2, 3))
        else:
            stacked = jnp.stack(new_p[i])
        outs.append(stacked)
        outs.append(jnp.stack(new_s[i]))
    return tuple(outs)
```

```python
import functools
import math

import numpy as np
import jax
import jax.numpy as jnp
from jax import lax
from jax.experimental import pallas as pl
from jax.experimental.pallas import tpu as pltpu

D_MODEL = 1024
DEPTH = 4
CHUNK_SHIFT = 6
EPS = 1e-6
D_FF = 4 * D_MODEL
H_A = 6
NOPE_D = 64
ROPE_D = 32
V_D = 64
QK_D = NOPE_D + ROPE_D
Q_LORA = 256
KV_LORA = 128
ROPE_BASE = 10000.0
H_B = 6
HD_B = 64
H_IDX = 8
D_IDX = 32
TOPK_MAX = 256
S5_CH = D_MODEL // 4
S5_GROUP = 16
S5_G = S5_CH // S5_GROUP
S5_P = 64
S5_N = S5_G * S5_P
IN_SIZES = (Q_LORA, KV_LORA, ROPE_D, H_B * HD_B, H_B * HD_B, H_B * HD_B, H_IDX * D_IDX, D_IDX, H_IDX, S5_CH)

LANE = 128
BF = jnp.bfloat16
F32 = jnp.float32
NEG = -1e30
INT_MIN = -2 ** 31
VMEM_LIMIT = 56 * 1024 * 1024
CAUSAL_SEG = 256
KEY_BLOCK = 256
KEY_OF_NEG_INF = -2139095041
NO_TIE_LIMIT = 2 ** 30

SEG_CQ, SEG_CKV, SEG_KR, SEG_BQ, SEG_BK, SEG_BV, SEG_IQ, SEG_IK, SEG_IW, SEG_U = (
    0, 256, 384, 512, 896, 1280, 1664, 1920, 2048, 2176)
IN_COLS_PADDED = 2432


def _dot(a, b):
    return jnp.dot(a.astype(BF), b.astype(BF), preferred_element_type=F32)


def _dot_nt(a, b):
    return lax.dot_general(a.astype(BF), b.astype(BF), (((1,), (1,)), ((), ())), preferred_element_type=F32)


def _rs(x, n):
    return lax.rsqrt(jnp.sum(x * x, axis=-1, keepdims=True) * (1.0 / n) + EPS)


def _half_rs(x, lo, n_lo, n_hi):
    sq = x * x
    rs_lo = lax.rsqrt(jnp.sum(jnp.where(lo, sq, 0.0), axis=-1, keepdims=True) * (1.0 / n_lo) + EPS)
    rs_hi = lax.rsqrt(jnp.sum(jnp.where(lo, 0.0, sq), axis=-1, keepdims=True) * (1.0 / n_hi) + EPS)
    return jnp.where(lo, rs_lo, rs_hi)


def _rope(x, tab_ref):
    return x * tab_ref[0] + pltpu.roll(x, LANE - ROPE_D // 2, 1) * tab_ref[1] + pltpu.roll(x, ROPE_D // 2, 1) * tab_ref[2]


def _params(sem):
    return pltpu.CompilerParams(dimension_semantics=sem, vmem_limit_bytes=VMEM_LIMIT)


def _ada_body(c_ref, w_ref, b_ref, o_ref):
    c = c_ref[...]
    s = c * (1.0 / (1.0 + jnp.exp(-c)))
    o_ref[0] = _dot(s, w_ref[0]) + b_ref[0]


def _ada(c_all, ada_w, ada_b):
    n = c_all.shape[0]
    tn = 1536
    return pl.pallas_call(
        _ada_body,
        grid=(DEPTH, 6 * D_MODEL // tn),
        in_specs=[pl.BlockSpec((n, D_MODEL), lambda l, j: (0, 0)),
                  pl.BlockSpec((1, D_MODEL, tn), lambda l, j: (l, 0, j)),
                  pl.BlockSpec((1, 1, tn), lambda l, j: (l, 0, j))],
        out_specs=pl.BlockSpec((1, n, tn), lambda l, j: (l, 0, j)),
        out_shape=jax.ShapeDtypeStruct((DEPTH, n, 6 * D_MODEL), F32),
        compiler_params=_params(("arbitrary", "arbitrary")),
        name="ada_mod",
    )(c_all, ada_w, ada_b.reshape(DEPTH, 1, 6 * D_MODEL))


def _proj_body(kv_transposed, x_ref, sc_ref, sh_ref, gn_ref, win_ref, gq_ref, wuq_ref, gqs_ref, tq_ref, gkv_ref, gkr_ref,
               tk_ref, gdq_ref, gdk_ref, tile_ref, *rest):
    q_o, ckv_o, kr_o, dq_o, dk_o, dkb_o, dv_o, dvb_o, iq_o, ik_o, ikt_o, iw_o, u_o = rest[-13:]
    x = x_ref[0]
    tm = x.shape[0]
    xn = x * _rs(x, D_MODEL)
    h = xn * gn_ref[...] * (1.0 + sc_ref[0]) + sh_ref[0]
    z = _dot(h, win_ref[...])
    lo = lax.broadcasted_iota(jnp.int32, (tm, LANE), 1) < (LANE // 2)

    cq = z[:, SEG_CQ:SEG_CQ + Q_LORA]
    cqn = cq * _rs(cq, Q_LORA) * gq_ref[...]
    q = _dot(cqn, wuq_ref[...])
    for hh in range(H_A):
        qs = q[:, hh * LANE:(hh + 1) * LANE]
        qn = qs * _half_rs(qs, lo, NOPE_D, ROPE_D) * gqs_ref[...]
        q_o[0, :, hh * LANE:(hh + 1) * LANE] = (_rope(qn, tq_ref) * (QK_D ** -0.5)).astype(BF)

    ckv = z[:, SEG_CKV:SEG_CKV + KV_LORA]
    ckv_o[0] = ckv * _rs(ckv, KV_LORA) * gkv_ref[...]
    krs = z[:, SEG_KR:SEG_KR + LANE]
    krn = krs * _rs(krs, ROPE_D) * gkr_ref[...]
    kr_o[0] = _rope(krn, tk_ref)[:, 0:ROPE_D]

    for j in range(H_B // 2):
        bq = z[:, SEG_BQ + j * LANE:SEG_BQ + (j + 1) * LANE]
        qn = bq * _half_rs(bq, lo, HD_B, HD_B) * gdq_ref[...] * (HD_B ** -0.5)
        bk = z[:, SEG_BK + j * LANE:SEG_BK + (j + 1) * LANE]
        kn = bk * _half_rs(bk, lo, HD_B, HD_B) * gdk_ref[...]
        bv = z[:, SEG_BV + j * LANE:SEG_BV + (j + 1) * LANE]
        dkb_o[0, :, j * LANE:(j + 1) * LANE] = kn.astype(BF)
        if kv_transposed:
            bvt = jnp.transpose(bv)
            dq_o[0, j * LANE:(j + 1) * LANE, :] = jnp.transpose(qn).astype(BF)
            dk_o[0, 0, j * LANE:(j + 1) * LANE, :] = jnp.transpose(kn)
            dv_o[0, 0, j * LANE:(j + 1) * LANE, :] = bvt
            dvb_o[0, j * LANE:(j + 1) * LANE, :] = bvt.astype(BF)
        else:
            dq_o[0, :, j * LANE:(j + 1) * LANE] = qn.astype(BF)
            dk_o[0, :, j * LANE:(j + 1) * LANE] = kn
            dv_o[0, :, j * LANE:(j + 1) * LANE] = bv
            dvb_o[0, :, j * LANE:(j + 1) * LANE] = bv.astype(BF)
    iqs = z[:, SEG_IQ:SEG_IQ + H_IDX * D_IDX] * (D_IDX ** -0.5)
    ik = z[:, SEG_IK:SEG_IK + D_IDX]
    iw = z[:, SEG_IW:SEG_IW + LANE] * (H_IDX ** -0.5)
    ik_o[0] = ik
    if kv_transposed:
        for c in range(H_IDX * D_IDX // LANE):
            iq_o[0, c * LANE:(c + 1) * LANE, :] = jnp.transpose(iqs[:, c * LANE:(c + 1) * LANE]).astype(BF)
        ikt_o[0] = ik.astype(BF)
        iw_o[0] = jnp.transpose(iw)[0:H_IDX, :]
    else:
        iq_o[0] = iqs.astype(BF)
        ikt_o[0] = _dot(ik, tile_ref[...]).astype(BF)
        iw_o[0] = iw
    u_o[0] = z[:, SEG_U:SEG_U + S5_CH]


def _proj(x, sc, sh, tabq, tabk, w, tm, kv_stack=None, layer=0):
    g, r, _ = x.shape
    per_token = sc.shape[1] != 1
    mod_spec = (pl.BlockSpec((1, tm, D_MODEL), lambda b, i: (b, i, 0)) if per_token
                else pl.BlockSpec((1, 1, D_MODEL), lambda b, i: (b, 0, 0)))
    const = lambda shape: pl.BlockSpec(shape, lambda b, i: tuple(0 for _ in shape))
    tab_spec = pl.BlockSpec((3, tm, LANE), lambda b, i: (0, i, 0))
    rows = lambda n, dt: (pl.BlockSpec((1, tm, n), lambda b, i: (b, i, 0)), jax.ShapeDtypeStruct((g, r, n), dt))
    cols = lambda n, dt: (pl.BlockSpec((1, n, tm), lambda b, i: (b, 0, i)), jax.ShapeDtypeStruct((g, n, r), dt))
    kv = 384
    nidx = H_IDX * D_IDX
    if kv_stack is None:
        dq, dk, dv, dvb = rows(kv, BF), rows(kv, F32), rows(kv, F32), rows(kv, BF)
        iq, ikt, iw = rows(nidx, BF), rows(nidx, BF), rows(LANE, F32)
    else:
        stack = (pl.BlockSpec((1, 1, kv, tm), lambda b, i: (layer, b, 0, i)),
                 jax.ShapeDtypeStruct((DEPTH, g, kv, r), F32))
        dq, dk, dv, dvb = cols(kv, BF), stack, stack, cols(kv, BF)
        iq, ikt, iw = cols(nidx, BF), rows(D_IDX, BF), cols(H_IDX, F32)
    outs = [rows(H_A * LANE, BF), rows(KV_LORA, F32), rows(ROPE_D, F32), dq, dk, rows(kv, BF), dv, dvb,
            iq, rows(D_IDX, F32), ikt, iw, rows(S5_CH, F32)]
    in_specs = [pl.BlockSpec((1, tm, D_MODEL), lambda b, i: (b, i, 0)), mod_spec, mod_spec,
                const((1, D_MODEL)), const((D_MODEL, IN_COLS_PADDED)), const((1, Q_LORA)),
                const((Q_LORA, H_A * LANE)), const((1, LANE)), tab_spec, const((1, KV_LORA)), const((1, LANE)),
                tab_spec, const((1, LANE)), const((1, LANE)), const((D_IDX, H_IDX * D_IDX))]
    args = [x, sc, sh, w["norm1_g"], w["w_in"], w["mla_gq"], w["mla_wuq"], w["mla_gq_slot"], tabq, w["mla_gkv"],
            w["mla_gkr"], tabk, w["dsa_gq"], w["dsa_gk"], w["idx_tile"]]
    aliases = {}
    if kv_stack is not None and kv_stack != "new":
        aliases = {len(args): 4, len(args) + 1: 6}
        in_specs += [pl.BlockSpec(memory_space=pl.ANY), pl.BlockSpec(memory_space=pl.ANY)]
        args += list(kv_stack)
    return pl.pallas_call(
        functools.partial(_proj_body, kv_stack is not None),
        grid=(g, r // tm),
        in_specs=in_specs,
        out_specs=[o[0] for o in outs],
        out_shape=[o[1] for o in outs],
        input_output_aliases=aliases,
        compiler_params=_params(("arbitrary", "arbitrary")),
        name="in_proj",
    )(*args)


def _kvup_body(v_transposed, ckv_ref, kr_ref, wk_ref, wv_ref, gkn_ref, place_ref, k_o, v_o):
    ckv = ckv_ref[0]
    krp = jnp.dot(kr_ref[...], place_ref[...], precision=lax.Precision.HIGHEST, preferred_element_type=F32)
    kn = _dot(ckv, wk_ref[...])
    for hh in range(H_A):
        ks = kn[:, hh * LANE:(hh + 1) * LANE]
        k_o[:, hh * LANE:(hh + 1) * LANE] = (ks * _rs(ks, NOPE_D) * gkn_ref[...] + krp).astype(BF)
    v = _dot(ckv, wv_ref[...])
    if v_transposed:
        for j in range(H_A * V_D // LANE):
            v_o[0, j * LANE:(j + 1) * LANE, :] = jnp.transpose(v[:, j * LANE:(j + 1) * LANE]).astype(BF)
    else:
        v_o[...] = v.astype(BF)


def _kvup(ckv, kr, w, tm, layer=0, batch_len=None):
    n = ckv.shape[1]
    const = lambda shape: pl.BlockSpec(shape, lambda i: tuple(0 for _ in shape))
    nv = H_A * V_D
    if batch_len is None:
        v_spec, v_shape = pl.BlockSpec((tm, nv), lambda i: (i, 0)), jax.ShapeDtypeStruct((n, nv), BF)
    else:
        per = batch_len // tm
        v_spec = pl.BlockSpec((1, nv, tm), lambda i: (i // per, 0, i % per))
        v_shape = jax.ShapeDtypeStruct((n // batch_len, nv, batch_len), BF)
    return pl.pallas_call(
        functools.partial(_kvup_body, batch_len is not None),
        grid=(n // tm,),
        in_specs=[pl.BlockSpec((1, tm, KV_LORA), lambda i: (layer, i, 0)), pl.BlockSpec((tm, ROPE_D), lambda i: (i, 0)),
                  const((KV_LORA, H_A * LANE)), const((KV_LORA, nv)), const((1, LANE)), const((ROPE_D, LANE))],
        out_specs=[pl.BlockSpec((tm, H_A * LANE), lambda i: (i, 0)), v_spec],
        out_shape=[jax.ShapeDtypeStruct((n, H_A * LANE), BF), v_shape],
        compiler_params=_params(("arbitrary",)),
        name="mla_kv_up",
    )(ckv, kr, w["mla_wk"], w["mla_wv"], w["mla_gkn_slot"], w["kr_place"])


def _visible(tq, sp, q0, s_valid):
    qpos = q0 + lax.broadcasted_iota(jnp.int32, (tq, sp), 0)
    kpos = lax.broadcasted_iota(jnp.int32, (tq, sp), 1)
    vis = lax.shift_right_logical(kpos, CHUNK_SHIFT) <= lax.shift_right_logical(qpos, CHUNK_SHIFT)
    if s_valid < sp:
        vis = jnp.logical_and(vis, kpos < s_valid)
    return vis, qpos, kpos


def _gather_keys(past_ref, new_ref, all_ref, p, t):
    sp = all_ref.shape[0]
    all_ref[0:p, :] = past_ref[0].astype(all_ref.dtype)
    all_ref[p:p + t, :] = new_ref[0].astype(all_ref.dtype)
    if p + t < sp:
        all_ref[p + t:sp, :] = jnp.zeros((sp - p - t, all_ref.shape[1]), all_ref.dtype)
    return all_ref[...]


def _causal_variants(t, tq, compute):
    seg = min(CAUSAL_SEG, t)
    i = pl.program_id(1)
    for sg in range(t // seg):
        first, last = sg * seg // tq, (sg + 1) * seg // tq
        pl.when(jnp.logical_and(i >= first, i < last))(functools.partial(compute, (sg + 1) * seg))


def _mla_body(p, t, tq, q_ref, k_ref, v_ref, *rest):
    if p:
        kp_ref, vp_ref, o_ref, kall, vall = rest
        k_all = _gather_keys(kp_ref, k_ref, kall, p, t)
        v_all = _gather_keys(vp_ref, v_ref, vall, p, t)
        _mla_compute(p, t, tq, q_ref, lambda n: k_all, lambda n: v_all, o_ref, k_all.shape[0])
    else:
        (o_ref,) = rest
        _causal_variants(t, tq, functools.partial(
            _mla_compute, p, t, tq, q_ref, lambda n: k_ref[0, 0:n, :], lambda n: v_ref[0, 0:n, :], o_ref))


def _mla_compute(p, t, tq, q_ref, load_k, load_v, o_ref, svis):
    k = load_k(svis)
    v = load_v(svis)
    q = q_ref[0]
    vis, _, _ = _visible(tq, svis, p + pl.program_id(1) * tq, p + t)
    lo = lax.broadcasted_iota(jnp.int32, (tq, LANE), 1) < (LANE // 2)
    for j in range(H_A // 2):
        vpair = v[:, j * LANE:(j + 1) * LANE]
        outs = []
        for e in range(2):
            hh = 2 * j + e
            s = _dot_nt(q[:, hh * LANE:(hh + 1) * LANE], k[:, hh * LANE:(hh + 1) * LANE])
            s = jnp.where(vis, s, NEG)
            pr = jnp.exp(s - jnp.max(s, axis=-1, keepdims=True))
            outs.append(_dot(pr, vpair) / jnp.sum(pr, axis=-1, keepdims=True))
        o_ref[0, :, j * LANE:(j + 1) * LANE] = jnp.where(lo, outs[0], outs[1]).astype(BF)


def _mla(q, k, v, kp, vp, tq):
    b, t, _ = q.shape
    p = 0 if kp is None else kp.shape[1]
    sp = -(-(p + t) // LANE) * LANE
    batch = lambda n, w: pl.BlockSpec((1, n, w), lambda bi, i: (bi, 0, 0))
    in_specs = [pl.BlockSpec((1, tq, H_A * LANE), lambda bi, i: (bi, i, 0)), batch(t, H_A * LANE), batch(t, H_A * V_D)]
    args = [q, k, v]
    scratch = []
    if p:
        in_specs += [batch(p, H_A * LANE), batch(p, H_A * V_D)]
        args += [kp, vp]
        scratch = [pltpu.VMEM((sp, H_A * LANE), BF), pltpu.VMEM((sp, H_A * V_D), BF)]
    return pl.pallas_call(
        functools.partial(_mla_body, p, t, tq),
        grid=(b, t // tq),
        in_specs=in_specs,
        out_specs=pl.BlockSpec((1, tq, H_A * V_D), lambda bi, i: (bi, i, 0)),
        out_shape=jax.ShapeDtypeStruct((b, t, H_A * V_D), BF),
        scratch_shapes=scratch,
        compiler_params=_params(("arbitrary", "arbitrary")),
        name="mla_attn",
    )(*args)


def _key_to_f32(key):
    return lax.bitcast_convert_type(jnp.where(key >= 0, key, key ^ 0x7FFFFFFF), F32)


def _kth_largest(count_ge, shape, kk):
    t0 = jnp.where(count_ge(jnp.zeros(shape, F32)) >= kk, 0, INT_MIN).astype(jnp.int32)

    def bit_step(i, tk):
        cand = tk + lax.shift_left(jnp.int32(1), 30 - i)
        return jnp.where(count_ge(_key_to_f32(cand)) >= kk, cand, tk)

    tk = lax.fori_loop(0, 31, bit_step, t0)
    return _key_to_f32(jnp.maximum(tk, KEY_OF_NEG_INF))


def _kth_largest_two_phase(count_ge_bf16, count_ge, shape, kk):
    def grid_f32(k16):
        bits16 = jnp.where(k16 >= 0, k16, k16 ^ 0x7FFF)
        return lax.bitcast_convert_type(lax.shift_left(bits16, 16), F32)

    t16 = jnp.where(count_ge_bf16(jnp.zeros(shape, F32)) >= kk, 0, -32768).astype(jnp.int32)

    def coarse_step(i, tk):
        cand = tk + lax.shift_left(jnp.int32(1), 14 - i)
        return jnp.where(count_ge_bf16(grid_f32(cand)) >= kk, cand, tk)

    t16 = lax.fori_loop(0, 15, coarse_step, t16)
    base = lax.shift_left(t16, 16) + jnp.where(t16 < 0, 0xFFFF, 0) - 0x8000

    def fine_step(i, off):
        cand = off + lax.shift_left(jnp.int32(1), 16 - i)
        return jnp.where(count_ge(_key_to_f32(base + cand)) >= kk, cand, off)

    off = lax.fori_loop(0, 17, fine_step, jnp.zeros(shape, jnp.int32))
    return _key_to_f32(jnp.maximum(base + off, KEY_OF_NEG_INF))


def _tie_index_limit(count, count_tied_below, thr, kk, n_keys):
    n_gt = count(lambda x: x > thr)
    need = kk - n_gt
    excess = (count(lambda x: x >= thr) - n_gt) > need

    def bisect():
        def step(i, lim):
            cand = lim + lax.shift_left(jnp.int32(1), (n_keys - 1).bit_length() - 1 - i)
            return jnp.where(count_tied_below(cand) < need, cand, lim)

        lim = lax.fori_loop(0, (n_keys - 1).bit_length(), step, jnp.zeros(thr.shape, jnp.int32))
        return jnp.where(excess, lim, NO_TIE_LIMIT)

    any_excess = jnp.max(jnp.where(excess, 1.0, 0.0)) > 0.5
    return lax.cond(any_excess, bisect, lambda: jnp.full(thr.shape, NO_TIE_LIMIT, jnp.int32))


def _visible_block(kb, q0, tq, svis):
    if (kb + 1) * KEY_BLOCK <= svis - CAUSAL_SEG:
        return None
    kpos = kb * KEY_BLOCK + lax.broadcasted_iota(jnp.int32, (KEY_BLOCK, tq), 0)
    qpos = q0 + lax.broadcasted_iota(jnp.int32, (KEY_BLOCK, tq), 1)
    return lax.shift_right_logical(kpos, CHUNK_SHIFT) <= lax.shift_right_logical(qpos, CHUNK_SHIFT)


def _block_max(m, s):
    return jnp.maximum(m, jnp.max(s.reshape(KEY_BLOCK // 8, 8, s.shape[1]), axis=0))


def _softmax_pv(n_heads, m_acc, s_ref, p_ref, vt_ref, o_ref, svis):
    tq = p_ref.shape[1]
    nkb = svis // KEY_BLOCK
    for j in range(n_heads // 2):
        halves = []
        for e in range(2):
            hh = 2 * j + e
            m = jnp.max(m_acc[hh], axis=0, keepdims=True)
            l_acc = jnp.zeros((4, 8, tq), F32)
            for kb in range(nkb):
                rows = slice(kb * KEY_BLOCK, (kb + 1) * KEY_BLOCK)
                pr = jnp.exp(s_ref[hh, rows, :] - m)
                l_acc = l_acc + jnp.sum(pr.reshape(4, KEY_BLOCK // 32, 8, tq), axis=1)
                p_ref[rows, :] = pr.astype(BF)
            out_t = jnp.dot(vt_ref[0, j * LANE:(j + 1) * LANE, 0:svis], p_ref[0:svis, :], preferred_element_type=F32)
            out_t = out_t / jnp.sum(jnp.sum(l_acc, axis=0), axis=0, keepdims=True)
            halves.append(out_t[e * (LANE // 2):(e + 1) * (LANE // 2), :])
        o_ref[0, :, j * LANE:(j + 1) * LANE] = jnp.transpose(jnp.concatenate(halves, axis=0)).astype(BF)


def _mla_t_body(t, tq, q_ref, k_ref, vt_ref, o_ref, s_ref, p_ref):
    _causal_variants(t, tq, functools.partial(_mla_t_compute, tq, q_ref, k_ref, vt_ref, o_ref, s_ref, p_ref))


def _mla_t_compute(tq, q_ref, k_ref, vt_ref, o_ref, s_ref, p_ref, svis):
    q0 = pl.program_id(1) * tq
    q = q_ref[0]
    m_acc = [jnp.full((8, tq), NEG, F32) for _ in range(H_A)]
    for kb in range(svis // KEY_BLOCK):
        rows = slice(kb * KEY_BLOCK, (kb + 1) * KEY_BLOCK)
        vis = _visible_block(kb, q0, tq, svis)
        for hh in range(H_A):
            s = _dot_nt(k_ref[0, rows, hh * LANE:(hh + 1) * LANE], q[:, hh * LANE:(hh + 1) * LANE])
            if vis is not None:
                s = jnp.where(vis, s, NEG)
            s_ref[hh, rows, :] = s
            m_acc[hh] = _block_max(m_acc[hh], s)
    _softmax_pv(H_A, m_acc, s_ref, p_ref, vt_ref, o_ref, svis)


def _mla_prompt(q, k, vt, tq):
    b, t, _ = q.shape
    return pl.pallas_call(
        functools.partial(_mla_t_body, t, tq),
        grid=(b, t // tq),
        in_specs=[pl.BlockSpec((1, tq, H_A * LANE), lambda bi, i: (bi, i, 0)),
                  pl.BlockSpec((1, t, H_A * LANE), lambda bi, i: (bi, 0, 0)),
                  pl.BlockSpec((1, H_A * V_D, t), lambda bi, i: (bi, 0, 0))],
        out_specs=pl.BlockSpec((1, tq, H_A * V_D), lambda bi, i: (bi, i, 0)),
        out_shape=jax.ShapeDtypeStruct((b, t, H_A * V_D), BF),
        scratch_shapes=[pltpu.VMEM((H_A, t, tq), F32), pltpu.VMEM((t, tq), BF)],
        compiler_params=_params(("arbitrary", "arbitrary")),
        name="mla_prompt",
    )(q, k, vt)


def _dsa_t_body(t, tq, top, qt_ref, iqt_ref, iwt_ref, k_ref, vt_ref, ik_ref, o_ref, sc_ref, sc16_ref, s_ref, p_ref):
    _causal_variants(t, tq, functools.partial(_dsa_t_compute, tq, top, qt_ref, iqt_ref, iwt_ref, k_ref, vt_ref, ik_ref,
                                              o_ref, sc_ref, sc16_ref, s_ref, p_ref))


def _dsa_t_compute(tq, top, qt_ref, iqt_ref, iwt_ref, k_ref, vt_ref, ik_ref, o_ref, sc_ref, sc16_ref, s_ref, p_ref,
                   svis):
    q0 = pl.program_id(1) * tq
    nkb = svis // KEY_BLOCK

    iqt = iqt_ref[0]
    iwt = iwt_ref[0]
    for kb in range(nkb):
        rows = slice(kb * KEY_BLOCK, (kb + 1) * KEY_BLOCK)
        ikb = ik_ref[0, rows, :]
        acc = jnp.zeros((KEY_BLOCK, tq), F32)
        for hh in range(H_IDX):
            dots = jnp.dot(ikb, iqt[hh * D_IDX:(hh + 1) * D_IDX, :], preferred_element_type=F32)
            acc = acc + iwt[hh:hh + 1, :] * jnp.maximum(dots, 0.0)
        vis = _visible_block(kb, q0, tq, svis)
        if vis is not None:
            acc = jnp.where(vis, acc, -jnp.inf)
        sc_ref[rows, :] = acc
        sc16_ref[rows, :] = acc.astype(BF)

    def count(pred):
        x = sc_ref[0:svis, :].reshape(8, svis // 64, 8, tq)
        part = jnp.sum(jnp.where(pred(x), 1.0, 0.0), axis=1)
        return jnp.sum(jnp.sum(part, axis=0), axis=0, keepdims=True)

    def count_ge(cand):
        return count(lambda x: x >= cand)

    def count_ge_bf16(cand):
        x = sc16_ref[0:svis, :].reshape(8, svis // 128, 16, tq)
        c = cand.astype(BF)
        one, zero = jnp.ones((8, 16, tq), BF), jnp.zeros((8, 16, tq), BF)
        part = zero
        for i in range(svis // 128):
            part = part + jnp.where(x[:, i] >= c, one, zero)
        return jnp.sum(jnp.sum(part.astype(F32), axis=0), axis=0, keepdims=True)

    thr = _kth_largest_two_phase(count_ge_bf16, count_ge, (1, tq), float(top))

    def count_tied_below(limit):
        acc = jnp.zeros((8, tq), F32)
        for kb in range(nkb):
            x = sc_ref[kb * KEY_BLOCK:(kb + 1) * KEY_BLOCK, :]
            kpos = kb * KEY_BLOCK + lax.broadcasted_iota(jnp.int32, (KEY_BLOCK, tq), 0)
            hit = jnp.where(x == thr, jnp.where(kpos < limit, 1.0, 0.0), 0.0)
            acc = acc + jnp.sum(hit.reshape(KEY_BLOCK // 8, 8, tq), axis=0)
        return jnp.sum(acc, axis=0, keepdims=True)

    tie_limit = _tie_index_limit(count, count_tied_below, thr, float(top), svis)

    qt = qt_ref[0]
    zeros = jnp.zeros((HD_B, tq), BF)
    qms = []
    for hh in range(H_B):
        qh = qt[hh * HD_B:(hh + 1) * HD_B, :]
        qms.append(jnp.concatenate([qh, zeros] if hh % 2 == 0 else [zeros, qh], axis=0))
    m_acc = [jnp.full((8, tq), NEG, F32) for _ in range(H_B)]
    qpos = q0 + lax.broadcasted_iota(jnp.int32, (KEY_BLOCK, tq), 1)
    for kb in range(nkb):
        rows = slice(kb * KEY_BLOCK, (kb + 1) * KEY_BLOCK)
        kpos = kb * KEY_BLOCK + lax.broadcasted_iota(jnp.int32, (KEY_BLOCK, tq), 0)
        x = sc_ref[rows, :]
        bias = jnp.where(x > thr, 0.0, jnp.where(x == thr, jnp.where(kpos <= tie_limit, 0.0, NEG), NEG))
        vis = _visible_block(kb, q0, tq, svis)
        if vis is not None:
            bias = jnp.where(vis, bias, NEG)
        dist = jnp.abs(qpos - kpos).astype(F32)
        for hh in range(H_B):
            kpair = k_ref[0, rows, (hh // 2) * LANE:(hh // 2 + 1) * LANE]
            s = jnp.dot(kpair, qms[hh], preferred_element_type=F32) - (2.0 ** (-8.0 * (hh + 1) / H_B)) * dist + bias
            s_ref[hh, rows, :] = s
            m_acc[hh] = _block_max(m_acc[hh], s)
    _softmax_pv(H_B, m_acc, s_ref, p_ref, vt_ref, o_ref, svis)


def _dsa_prompt(qt, iqt, iwt, k, vt, ik, tq):
    b, _, t = qt.shape
    top = min(TOPK_MAX, t // 4)
    qcols = lambda n: pl.BlockSpec((1, n, tq), lambda bi, i: (bi, 0, i))
    batch = lambda n, wd: pl.BlockSpec((1, n, wd), lambda bi, i: (bi, 0, 0))
    return pl.pallas_call(
        functools.partial(_dsa_t_body, t, tq, top),
        grid=(b, t // tq),
        in_specs=[qcols(384), qcols(H_IDX * D_IDX), qcols(H_IDX), batch(t, 384), batch(384, t), batch(t, D_IDX)],
        out_specs=pl.BlockSpec((1, tq, 384), lambda bi, i: (bi, i, 0)),
        out_shape=jax.ShapeDtypeStruct((b, t, 384), BF),
        scratch_shapes=[pltpu.VMEM((t, tq), F32), pltpu.VMEM((t, tq), BF), pltpu.VMEM((H_B, t, tq), F32),
                        pltpu.VMEM((t, tq), BF)],
        compiler_params=_params(("arbitrary", "arbitrary")),
        name="dsa_attn",
    )(qt, iqt, iwt, k, vt, ik)


def _dsa_s_body(p, t, top, q_ref, iq_ref, iw_ref, k_ref, v_ref, ikt_ref, kpt_ref, vpt_ref, ikpt_ref, o_ref, sc_ref):
    sp = sc_ref.shape[1]
    pad_rows = lambda x: jnp.concatenate([x, jnp.zeros((LANE - t, x.shape[1]), x.dtype)], axis=0)
    lane = lax.broadcasted_iota(jnp.int32, (t, sp), 1)
    qrow = p + lax.broadcasted_iota(jnp.int32, (t, sp), 0)
    valid = jnp.logical_and(
        lane < p + t, lax.shift_right_logical(lane, CHUNK_SHIFT) <= lax.shift_right_logical(qrow, CHUNK_SHIFT))

    iq = iq_ref[0]
    iw = iw_ref[0]
    ikp = ikpt_ref[0, 0].astype(BF)
    ikp_tiled = jnp.concatenate([ikp] * H_IDX, axis=0)
    ikn = pad_rows(ikt_ref[0])
    head_of_lane = lax.shift_right_logical(lax.broadcasted_iota(jnp.int32, (t, H_IDX * D_IDX), 1), 5)
    sc_p = jnp.zeros((t, p), F32)
    sc_n = jnp.zeros((t, LANE), F32)
    for hh in range(H_IDX):
        qh = jnp.where(head_of_lane == hh, iq, jnp.zeros_like(iq))
        wh = iw[:, hh:hh + 1]
        sc_p = sc_p + wh * jnp.maximum(jnp.dot(qh, ikp_tiled, preferred_element_type=F32), 0.0)
        sc_n = sc_n + wh * jnp.maximum(_dot_nt(qh, ikn), 0.0)
    sc_ref[...] = jnp.where(valid, jnp.concatenate([sc_p, sc_n], axis=1), -jnp.inf)

    def count(pred):
        return jnp.sum(jnp.where(pred(sc_ref[...]), 1.0, 0.0), axis=1, keepdims=True)

    thr = _kth_largest(lambda cand: count(lambda x: x >= cand), (t, 1), float(top))

    def count_tied_below(limit):
        hit = jnp.where(sc_ref[...] == thr, jnp.where(lane < limit, 1.0, 0.0), 0.0)
        return jnp.sum(hit, axis=1, keepdims=True)

    tie_limit = _tie_index_limit(count, count_tied_below, thr, float(top), sp)
    x = sc_ref[...]
    bias = jnp.where(x > thr, 0.0, jnp.where(x == thr, jnp.where(lane <= tie_limit, 0.0, NEG), NEG))
    bias = jnp.where(valid, bias, NEG)
    dist = jnp.abs(qrow - lane).astype(F32)

    q = q_ref[0]
    lo = lax.broadcasted_iota(jnp.int32, (t, LANE), 1) < (LANE // 2)
    for j in range(H_B // 2):
        cols = slice(j * LANE, (j + 1) * LANE)
        qpair = q[:, cols]
        kpt = kpt_ref[0, 0, cols, :].astype(BF)
        vpt = vpt_ref[0, 0, cols, :].astype(BF)
        kn = pad_rows(k_ref[0][:, cols])
        vn = pad_rows(v_ref[0][:, cols])
        outs = []
        for e in range(2):
            hh = 2 * j + e
            qh = jnp.where(lo if e == 0 else jnp.logical_not(lo), qpair, jnp.zeros_like(qpair))
            s = jnp.concatenate([jnp.dot(qh, kpt, preferred_element_type=F32), _dot_nt(qh, kn)], axis=1)
            s = s - (2.0 ** (-8.0 * (hh + 1) / H_B)) * dist + bias
            pr = jnp.exp(s - jnp.max(s, axis=-1, keepdims=True))
            prb = pr.astype(BF)
            o = _dot_nt(prb[:, 0:p], vpt) + jnp.dot(prb[:, p:sp], vn, preferred_element_type=F32)
            outs.append(o / jnp.sum(pr, axis=-1, keepdims=True))
        o_ref[0, :, cols] = jnp.where(lo, outs[0], outs[1]).astype(BF)


def _dsa_decode(q, iq, iw, k, v, ikt, kpt, vpt, ikpt, layer):
    b, t, _ = q.shape
    p = kpt.shape[-1]
    sp = p + LANE
    top = min(TOPK_MAX, (p + t) // 4)
    new = lambda wd: pl.BlockSpec((1, t, wd), lambda bi: (bi, 0, 0))
    cache = lambda n: pl.BlockSpec((1, 1, n, p), lambda bi: (layer, bi, 0, 0))
    return pl.pallas_call(
        functools.partial(_dsa_s_body, p, t, top),
        grid=(b,),
        in_specs=[new(384), new(256), new(LANE), new(384), new(384), new(256), cache(384), cache(384), cache(D_IDX)],
        out_specs=new(384),
        out_shape=jax.ShapeDtypeStruct((b, t, 384), BF),
        scratch_shapes=[pltpu.VMEM((t, sp), F32)],
        compiler_params=_params(("arbitrary",)),
        name="dsa_decode",
    )(q, iq, iw, k, v, ikt, kpt, vpt, ikpt)


def _s5_body(has_h0, nb, tt, u_ref, are_ref, aim_ref, ldt_ref, bre_ref, bim_ref, cre_ref, cim_ref, d_ref, wg_ref,
             bg_ref, *rest):
    if has_h0:
        h0re_ref, h0im_ref, o_ref, sre_o, sim_o, xre, xim = rest
    else:
        o_ref, sre_o, sim_o, xre, xim = rest

    @pl.when(pl.program_id(0) == 0)
    def _():
        if has_h0:
            sre_o[...] = h0re_ref[...]
            sim_o[...] = h0im_ref[...]
        else:
            sre_o[...] = jnp.zeros_like(sre_o)
            sim_o[...] = jnp.zeros_like(sim_o)

    ar = are_ref[...]
    ai = aim_ref[...]
    dt = jnp.exp(ldt_ref[...])
    mag = jnp.exp(dt * ar)
    ab_re = mag * jnp.cos(dt * ai)
    ab_im = mag * jnp.sin(dt * ai)
    den = ar * ar + ai * ai
    nr = ab_re - 1.0
    f_re = (nr * ar + ab_im * ai) / den
    f_im = (ab_im * ar - nr * ai) / den

    u = u_ref[...].reshape(tt * nb, S5_CH)
    bu_re = _dot(u, bre_ref[...])
    bu_im = _dot(u, bim_ref[...])
    xre[...] = f_re * bu_re - f_im * bu_im
    xim[...] = f_re * bu_im + f_im * bu_re

    a_re = jnp.broadcast_to(ab_re, (nb, S5_N))
    a_im = jnp.broadcast_to(ab_im, (nb, S5_N))

    def step(ti, carry):
        s_re, s_im = carry
        rows = pl.ds(pl.multiple_of(ti * nb, nb), nb)
        n_re = a_re * s_re - a_im * s_im + xre[rows, :]
        n_im = a_re * s_im + a_im * s_re + xim[rows, :]
        xre[rows, :] = n_re
        xim[rows, :] = n_im
        return n_re, n_im

    s_re, s_im = lax.fori_loop(0, tt, step, (sre_o[...], sim_o[...]), unroll=2)
    sre_o[...] = s_re
    sim_o[...] = s_im

    y = _dot(xre[...], cre_ref[...]) - _dot(xim[...], cim_ref[...]) + d_ref[...] * u
    g = _dot(y, wg_ref[...]) + bg_ref[...]
    out = g[:, 0:S5_CH] * (1.0 / (1.0 + jnp.exp(-g[:, S5_CH:2 * S5_CH])))
    o_ref[...] = out.reshape(tt, nb, S5_CH)


def _s5(u, h0, w, tt):
    nb, t, _ = u.shape
    u = jnp.swapaxes(u, 0, 1)
    const = lambda shape: pl.BlockSpec(shape, lambda i: tuple(0 for _ in shape))
    in_specs = [pl.BlockSpec((tt, nb, S5_CH), lambda i: (i, 0, 0)), const((1, S5_N)), const((1, S5_N)),
                const((1, S5_N)), const((S5_CH, S5_N)), const((S5_CH, S5_N)), const((S5_N, S5_CH)),
                const((S5_N, S5_CH)), const((1, S5_CH)), const((S5_CH, 2 * S5_CH)), const((1, 2 * S5_CH))]
    args = [u, w["s5_a_re"], w["s5_a_im"], w["s5_log_dt"], w["s5_bre"], w["s5_bim"], w["s5_cre"], w["s5_cim"],
            w["s5_d"], w["s5_w_glu"], w["s5_b_glu"]]
    if h0 is not None:
        in_specs += [const((nb, S5_N)), const((nb, S5_N))]
        args += [h0[0], h0[1]]
    out, s_re, s_im = pl.pallas_call(
        functools.partial(_s5_body, h0 is not None, nb, tt),
        grid=(t // tt,),
        in_specs=in_specs,
        out_specs=[pl.BlockSpec((tt, nb, S5_CH), lambda i: (i, 0, 0)), const((nb, S5_N)), const((nb, S5_N))],
        out_shape=[jax.ShapeDtypeStruct((t, nb, S5_CH), F32), jax.ShapeDtypeStruct((nb, S5_N), F32),
                   jax.ShapeDtypeStruct((nb, S5_N), F32)],
        scratch_shapes=[pltpu.VMEM((tt * nb, S5_N), F32), pltpu.VMEM((tt * nb, S5_N), F32)],
        compiler_params=_params(("arbitrary",)),
        name="s5_scan",
    )(*args)
    return jnp.swapaxes(out, 0, 1).astype(BF), s_re, s_im


def _out_body(a_ref, b_ref, c_ref, x_ref, g1_ref, sc_ref, sh_ref, g2_ref, gn_ref, wa_ref, wb_ref, wc_ref, w1_ref,
              w2_ref, o_ref):
    mix = _dot(a_ref[0], wa_ref[...]) + _dot(b_ref[0], wb_ref[...]) + _dot(c_ref[0], wc_ref[...])
    x1 = x_ref[0] + g1_ref[0] * mix
    h2 = (x1 * _rs(x1, D_MODEL) * gn_ref[...] * (1.0 + sc_ref[0]) + sh_ref[0]).astype(BF)
    ff = jnp.zeros_like(x1)
    for c in range(D_FF // D_MODEL):
        hid = jnp.maximum(_dot(h2, w1_ref[:, c * D_MODEL:(c + 1) * D_MODEL]), 0.0)
        ff = ff + _dot(hid * hid, w2_ref[c * D_MODEL:(c + 1) * D_MODEL, :])
    o_ref[0] = x1 + g2_ref[0] * ff


def _out(a, b, c, x, g1, sc, sh, g2, w, tm):
    g, r, _ = x.shape
    per_token = sc.shape[1] != 1
    mod_spec = (pl.BlockSpec((1, tm, D_MODEL), lambda bi, i: (bi, i, 0)) if per_token
                else pl.BlockSpec((1, 1, D_MODEL), lambda bi, i: (bi, 0, 0)))
    tile = lambda n: pl.BlockSpec((1, tm, n), lambda bi, i: (bi, i, 0))
    const = lambda shape: pl.BlockSpec(shape, lambda bi, i: tuple(0 for _ in shape), pipeline_mode=pl.Buffered(1))
    return pl.pallas_call(
        _out_body,
        grid=(g, r // tm),
        in_specs=[tile(384), tile(384), tile(S5_CH), tile(D_MODEL), mod_spec, mod_spec, mod_spec, mod_spec,
                  const((1, D_MODEL)), const((384, D_MODEL)), const((384, D_MODEL)), const((S5_CH, D_MODEL)),
                  const((D_MODEL, D_FF)), const((D_FF, D_MODEL))],
        out_specs=tile(D_MODEL),
        out_shape=jax.ShapeDtypeStruct((g, r, D_MODEL), F32),
        compiler_params=_params(("arbitrary", "arbitrary")),
        name="out_mlp",
    )(a, b, c, x, g1, sc, sh, g2, w["norm2_g"], w["w_out_a"], w["w_out_b"], w["w_out_c"], w["ff_w1"], w["ff_w2"])


def _slot_pad(v, lo, n):
    return jnp.pad(v, ((0, 0), (lo, LANE - lo - n)))


def _place_cols(wm, segments, total):
    parts, at = [], 0
    for dst, src, n in segments:
        if dst > at:
            parts.append(jnp.zeros(wm.shape[:-1] + (dst - at,), wm.dtype))
        parts.append(wm[..., src:src + n])
        at = dst + n
    if total > at:
        parts.append(jnp.zeros(wm.shape[:-1] + (total - at,), wm.dtype))
    return jnp.concatenate(parts, axis=-1)


def _prep_weights(p):
    f = {}
    offs = np.cumsum((0,) + IN_SIZES)
    starts = (SEG_CQ, SEG_CKV, SEG_KR, SEG_BQ, SEG_BK, SEG_BV, SEG_IQ, SEG_IK, SEG_IW, SEG_U)
    f["w_in"] = _place_cols(p["w_in"], [(s, int(o), n) for s, o, n in zip(starts, offs[:-1], IN_SIZES)],
                            IN_COLS_PADDED).astype(BF)
    kvw = NOPE_D + V_D
    f["mla_wuq"] = _place_cols(p["mla_wuq"], [(hh * LANE, hh * QK_D, QK_D) for hh in range(H_A)], H_A * LANE).astype(BF)
    f["mla_wk"] = _place_cols(p["mla_wukv"], [(hh * LANE, hh * kvw, NOPE_D) for hh in range(H_A)], H_A * LANE).astype(BF)
    f["mla_wv"] = _place_cols(p["mla_wukv"], [(hh * V_D, hh * kvw + NOPE_D, V_D) for hh in range(H_A)],
                              H_A * V_D).astype(BF)
    f["mla_gq_slot"] = jnp.concatenate([p["mla_gqn"], p["mla_gqr"], jnp.zeros((DEPTH, LANE - QK_D), F32)], axis=-1)
    f["mla_gkn_slot"] = _slot_pad(p["mla_gkn"], 0, NOPE_D)
    f["mla_gkr"] = _slot_pad(p["mla_gkr"], 0, ROPE_D)
    f["dsa_gq"] = jnp.concatenate([p["dsa_gq"], p["dsa_gq"]], axis=-1)
    f["dsa_gk"] = jnp.concatenate([p["dsa_gk"], p["dsa_gk"]], axis=-1)
    for name in ("norm1_g", "norm2_g", "mla_gq", "mla_gkv", "s5_d", "s5_b_glu"):
        f[name] = p[name]
    f["w_out_a"] = p["w_out"][:, 0:384].astype(BF)
    f["w_out_b"] = p["w_out"][:, 384:768].astype(BF)
    f["w_out_c"] = p["w_out"][:, 768:1024].astype(BF)
    f["ff_w1"] = p["ff_w1"].astype(BF)
    f["ff_w2"] = p["ff_w2"].astype(BF)
    f["s5_w_glu"] = p["s5_w_glu"].astype(BF)
    eye = jnp.eye(S5_G, dtype=F32)
    f["s5_bre"] = jnp.einsum("lgpc,gh->lgchp", p["s5_b_re"], eye).reshape(DEPTH, S5_CH, S5_N).astype(BF)
    f["s5_bim"] = jnp.einsum("lgpc,gh->lgchp", p["s5_b_im"], eye).reshape(DEPTH, S5_CH, S5_N).astype(BF)
    f["s5_cre"] = jnp.einsum("lgcp,gh->lgphc", p["s5_c_re"], eye).reshape(DEPTH, S5_N, S5_CH).astype(BF)
    f["s5_cim"] = jnp.einsum("lgcp,gh->lgphc", p["s5_c_im"], eye).reshape(DEPTH, S5_N, S5_CH).astype(BF)
    f["s5_a_re"] = p["s5_a_re"].reshape(DEPTH, S5_N)
    f["s5_a_im"] = p["s5_a_im"].reshape(DEPTH, S5_N)
    f["s5_log_dt"] = jnp.repeat(p["s5_log_dt"], S5_P, axis=-1)
    return f


def _layer_weights(f, l):
    w = {k: (v[l] if v.ndim == 3 else v[l][None, :]) for k, v in f.items()}
    place = np.zeros((ROPE_D, LANE), np.float32)
    place[np.arange(ROPE_D), NOPE_D + np.arange(ROPE_D)] = 1.0
    w["kr_place"] = jnp.asarray(place)
    tile = np.zeros((D_IDX, H_IDX * D_IDX), np.float32)
    for hh in range(H_IDX):
        tile[np.arange(D_IDX), hh * D_IDX + np.arange(D_IDX)] = 1.0
    w["idx_tile"] = jnp.asarray(tile, dtype=BF)
    return w


def _rope_tables(pos, lo):
    half = ROPE_D // 2
    inv = ROPE_BASE ** (-jnp.arange(half, dtype=F32) / half)
    ang = pos.astype(F32)[:, None] * inv[None, :]
    cos, sin = jnp.cos(ang), jnp.sin(ang)
    n = pos.shape[0]
    z = lambda w_: jnp.zeros((n, w_), F32)
    tail = LANE - lo - ROPE_D
    c = jnp.concatenate([jnp.ones((n, lo), F32), cos, cos, z(tail)], axis=-1)
    s1 = jnp.concatenate([z(lo), -sin, z(half), z(tail)], axis=-1)
    s2 = jnp.concatenate([z(lo), z(half), sin, z(tail)], axis=-1)
    return jnp.stack([c, s1, s2])


def _prompt_layer(x, mods, w, pos, tiles, layer, kv_stack):
    b, t, _ = x.shape
    tm, tq_mla, tq_dsa, tt = tiles
    sh1, sc1, g1, sh2, sc2, g2 = mods
    outs = _proj(x, sc1, sh1, _rope_tables(pos, NOPE_D), _rope_tables(pos, 0), w, tm, kv_stack, layer)
    q, ckv, kr, dqt, dk_stack, dkb, dv_stack, dvtb, iqt, ik, ikb, iwt, u = outs
    kn, vnt = _kvup(ckv.reshape(1, b * t, KV_LORA), kr.reshape(b * t, ROPE_D), w, min(512, t), batch_len=t)
    a_out = _mla_prompt(q, kn.reshape(b, t, H_A * LANE), vnt, tq_mla)
    b_out = _dsa_prompt(dqt, iqt, iwt, dkb, dvtb, ikb, tq_dsa)
    c_out, s_re, s_im = _s5(u, None, w, tt)
    y = _out(a_out, b_out, c_out, x, g1, sc2, sh2, g2, w, tm)
    state = (ckv, kr, None, None, ik, s_re.reshape(b, S5_G, S5_P), s_im.reshape(b, S5_G, S5_P))
    return y, state, (dk_stack, dv_stack)


def _decode_layer(x, mods, caches, w, pos, layer):
    b, t, _ = x.shape
    n = b * t
    c_ckv, c_kr, c_kt, c_vt, c_ikt, c_sre, c_sim = caches
    plen = c_kt.shape[-1]
    xf = x.reshape(1, n, D_MODEL)
    sh1, sc1, g1, sh2, sc2, g2 = (jnp.broadcast_to(m, (b, t, D_MODEL)).reshape(1, n, D_MODEL) for m in mods)
    tabq = jnp.tile(_rope_tables(pos, NOPE_D), (1, b, 1))
    tabk = jnp.tile(_rope_tables(pos, 0), (1, b, 1))
    outs = _proj(xf, sc1, sh1, tabq, tabk, w, n)
    q, ckv, kr, dq, dk, dkb, dv, dvb, iq, ik, ikt, iw, u = (o.reshape(b, t, o.shape[-1]) for o in outs)
    kn, vn = _kvup(ckv.reshape(1, n, KV_LORA), kr.reshape(n, ROPE_D), w, n)
    kp, vp = _kvup(c_ckv, c_kr[layer].reshape(b * plen, ROPE_D), w, min(512, b * plen), layer)
    a_out = _mla(q, kn.reshape(b, t, H_A * LANE), vn.reshape(b, t, H_A * V_D), kp.reshape(b, plen, H_A * LANE),
                 vp.reshape(b, plen, H_A * V_D), t)
    b_out = _dsa_decode(dq, iq, iw, dkb, dvb, ikt, c_kt, c_vt, c_ikt, layer)
    c_out, s_re, s_im = _s5(u, (c_sre[layer], c_sim[layer]), w, t)
    y = _out(a_out.reshape(1, n, -1), b_out.reshape(1, n, -1), c_out.reshape(1, n, -1), xf, g1, sc2, sh2, g2, w, n)
    state = (ckv, kr, dk.reshape(b, t, H_B, HD_B), dv.reshape(b, t, H_B, HD_B), ik,
             s_re.reshape(b, S5_G, S5_P), s_im.reshape(b, S5_G, S5_P))
    return y.reshape(b, t, D_MODEL), state


def kernel(x_prompt, x_sample, c_prompt, c_sample, cache_mla_ckv, cache_mla_krope, cache_dsa_k, cache_dsa_v, cache_dsa_idxk, state_s5_re, state_s5_im, ada_w, ada_b, norm1_g, norm2_g, w_in, w_out, mla_gq, mla_wuq, mla_gkv, mla_wukv, mla_gqn, mla_gqr, mla_gkn, mla_gkr, dsa_gq, dsa_gk, s5_a_re, s5_a_im, s5_b_re, s5_b_im, s5_c_re, s5_c_im, s5_d, s5_log_dt, s5_w_glu, s5_b_glu, ff_w1, ff_w2):
    params = dict(norm1_g=norm1_g, norm2_g=norm2_g, w_in=w_in, w_out=w_out, mla_gq=mla_gq, mla_wuq=mla_wuq,
                  mla_gkv=mla_gkv, mla_wukv=mla_wukv, mla_gqn=mla_gqn, mla_gqr=mla_gqr, mla_gkn=mla_gkn,
                  mla_gkr=mla_gkr, dsa_gq=dsa_gq, dsa_gk=dsa_gk, s5_a_re=s5_a_re, s5_a_im=s5_a_im, s5_b_re=s5_b_re,
                  s5_b_im=s5_b_im, s5_c_re=s5_c_re, s5_c_im=s5_c_im, s5_d=s5_d, s5_log_dt=s5_log_dt,
                  s5_w_glu=s5_w_glu, s5_b_glu=s5_b_glu, ff_w1=ff_w1, ff_w2=ff_w2)
    f = _prep_weights(params)
    nbp, tp, _ = x_prompt.shape
    nbs, ts, _ = x_sample.shape
    past_len = cache_mla_ckv.shape[2]
    mod = _ada(jnp.concatenate([c_prompt, c_sample], axis=0), ada_w, ada_b)
    pos_p = jnp.arange(tp, dtype=jnp.int32)
    pos_s = past_len + jnp.arange(ts, dtype=jnp.int32)
    caches = (cache_mla_ckv.reshape(DEPTH, nbs * past_len, KV_LORA), cache_mla_krope,
              jnp.transpose(cache_dsa_k, (0, 1, 3, 4, 2)).reshape(DEPTH, nbs, H_B * HD_B, past_len),
              jnp.transpose(cache_dsa_v, (0, 1, 3, 4, 2)).reshape(DEPTH, nbs, H_B * HD_B, past_len),
              jnp.transpose(cache_dsa_idxk, (0, 1, 3, 2)),
              state_s5_re.reshape(DEPTH, nbs, S5_N), state_s5_im.reshape(DEPTH, nbs, S5_N))
    xp, xs = x_prompt, x_sample
    new_p = [[] for _ in range(7)]
    new_s = [[] for _ in range(7)]
    kv_stack = "new"
    for l in range(DEPTH):
        w = _layer_weights(f, l)
        mods = [mod[l][:, None, i * D_MODEL:(i + 1) * D_MODEL] for i in range(6)]
        xp, st_p, kv_stack = _prompt_layer(xp, [m[:nbp] for m in mods], w, pos_p,
                                           (min(512, tp), min(128, tp), min(128, tp), min(128, tp)), l, kv_stack)
        xs, st_s = _decode_layer(xs, [m[nbp:] for m in mods], caches, w, pos_s, l)
        for i in range(7):
            new_p[i].append(st_p[i])
            new_s[i].append(st_s[i])
    outs = [xp, xs]
    for i in range(7):
        if i in (2, 3):
            stacked = jnp.transpose(kv_stack[i - 2].reshape(DEPTH, nbp, H_B, HD_B, tp), (0, 1, 4, 2, 3))
        else:
            stacked = jnp.stack(new_p[i])
        outs.append(stacked)
        outs.append(jnp.stack(new_s[i]))
    return tuple(outs)
```

```python
import functools
import math

import numpy as np
import jax
import jax.numpy as jnp
from jax import lax
from jax.experimental import pallas as pl
from jax.experimental.pallas import tpu as pltpu

D_MODEL = 1024
DEPTH = 4
CHUNK_SHIFT = 6
EPS = 1e-6
D_FF = 4 * D_MODEL
H_A = 6
NOPE_D = 64
ROPE_D = 32
V_D = 64
QK_D = NOPE_D + ROPE_D
Q_LORA = 256
KV_LORA = 128
ROPE_BASE = 10000.0
H_B = 6
HD_B = 64
H_IDX = 8
D_IDX = 32
TOPK_MAX = 256
S5_CH = D_MODEL // 4
S5_GROUP = 16
S5_G = S5_CH // S5_GROUP
S5_P = 64
S5_N = S5_G * S5_P
IN_SIZES = (Q_LORA, KV_LORA, ROPE_D, H_B * HD_B, H_B * HD_B, H_B * HD_B, H_IDX * D_IDX, D_IDX, H_IDX, S5_CH)

LANE = 128
BF = jnp.bfloat16
F32 = jnp.float32
NEG = -1e30
INT_MIN = -2 ** 31
VMEM_LIMIT = 56 * 1024 * 1024
CAUSAL_SEG = 256
KEY_BLOCK = 256
KEY_OF_NEG_INF = -2139095041
NO_TIE_LIMIT = 2 ** 30
DECODE_BATCHES_PER_STEP = 4

SEG_CQ, SEG_CKV, SEG_KR, SEG_BQ, SEG_BK, SEG_BV, SEG_IQ, SEG_IK, SEG_IW, SEG_U = (
    0, 256, 384, 512, 896, 1280, 1664, 1920, 2048, 2176)
IN_COLS_PADDED = 2432


def _dot(a, b):
    return jnp.dot(a.astype(BF), b.astype(BF), preferred_element_type=F32)


def _dot_nt(a, b):
    return lax.dot_general(a.astype(BF), b.astype(BF), (((1,), (1,)), ((), ())), preferred_element_type=F32)


def _rs(x, n):
    return lax.rsqrt(jnp.sum(x * x, axis=-1, keepdims=True) * (1.0 / n) + EPS)


def _half_rs(x, lo, n_lo, n_hi):
    sq = x * x
    rs_lo = lax.rsqrt(jnp.sum(jnp.where(lo, sq, 0.0), axis=-1, keepdims=True) * (1.0 / n_lo) + EPS)
    rs_hi = lax.rsqrt(jnp.sum(jnp.where(lo, 0.0, sq), axis=-1, keepdims=True) * (1.0 / n_hi) + EPS)
    return jnp.where(lo, rs_lo, rs_hi)


def _rope(x, tab_ref):
    return x * tab_ref[0] + pltpu.roll(x, LANE - ROPE_D // 2, 1) * tab_ref[1] + pltpu.roll(x, ROPE_D // 2, 1) * tab_ref[2]


def _params(sem):
    return pltpu.CompilerParams(dimension_semantics=sem, vmem_limit_bytes=VMEM_LIMIT)


def _ada_body(c_ref, w_ref, b_ref, o_ref):
    c = c_ref[...]
    s = c * (1.0 / (1.0 + jnp.exp(-c)))
    o_ref[0] = _dot(s, w_ref[0]) + b_ref[0]


def _ada(c_all, ada_w, ada_b):
    n = c_all.shape[0]
    tn = 1536
    return pl.pallas_call(
        _ada_body,
        grid=(DEPTH, 6 * D_MODEL // tn),
        in_specs=[pl.BlockSpec((n, D_MODEL), lambda l, j: (0, 0)),
                  pl.BlockSpec((1, D_MODEL, tn), lambda l, j: (l, 0, j)),
                  pl.BlockSpec((1, 1, tn), lambda l, j: (l, 0, j))],
        out_specs=pl.BlockSpec((1, n, tn), lambda l, j: (l, 0, j)),
        out_shape=jax.ShapeDtypeStruct((DEPTH, n, 6 * D_MODEL), F32),
        compiler_params=_params(("arbitrary", "arbitrary")),
        name="ada_mod",
    )(c_all, ada_w, ada_b.reshape(DEPTH, 1, 6 * D_MODEL))


def _proj_body(kv_transposed, x_ref, sc_ref, sh_ref, gn_ref, win_ref, gq_ref, wuq_ref, gqs_ref, tq_ref, gkv_ref, gkr_ref,
               tk_ref, gdq_ref, gdk_ref, tile_ref, *rest):
    q_o, ckv_o, kr_o, dq_o, dk_o, dkb_o, dv_o, dvb_o, iq_o, ik_o, ikt_o, iw_o, u_o = rest[-13:]
    x = x_ref[0]
    tm = x.shape[0]
    xn = x * _rs(x, D_MODEL)
    h = xn * gn_ref[...] * (1.0 + sc_ref[0]) + sh_ref[0]
    z = _dot(h, win_ref[...])
    lo = lax.broadcasted_iota(jnp.int32, (tm, LANE), 1) < (LANE // 2)

    cq = z[:, SEG_CQ:SEG_CQ + Q_LORA]
    cqn = cq * _rs(cq, Q_LORA) * gq_ref[...]
    q = _dot(cqn, wuq_ref[...])
    for hh in range(H_A):
        qs = q[:, hh * LANE:(hh + 1) * LANE]
        qn = qs * _half_rs(qs, lo, NOPE_D, ROPE_D) * gqs_ref[...]
        q_o[0, :, hh * LANE:(hh + 1) * LANE] = (_rope(qn, tq_ref) * (QK_D ** -0.5)).astype(BF)

    ckv = z[:, SEG_CKV:SEG_CKV + KV_LORA]
    ckv_o[0] = ckv * _rs(ckv, KV_LORA) * gkv_ref[...]
    krs = z[:, SEG_KR:SEG_KR + LANE]
    krn = krs * _rs(krs, ROPE_D) * gkr_ref[...]
    kr_o[0] = _rope(krn, tk_ref)[:, 0:ROPE_D]

    for j in range(H_B // 2):
        bq = z[:, SEG_BQ + j * LANE:SEG_BQ + (j + 1) * LANE]
        qn = bq * _half_rs(bq, lo, HD_B, HD_B) * gdq_ref[...] * (HD_B ** -0.5)
        bk = z[:, SEG_BK + j * LANE:SEG_BK + (j + 1) * LANE]
        kn = bk * _half_rs(bk, lo, HD_B, HD_B) * gdk_ref[...]
        bv = z[:, SEG_BV + j * LANE:SEG_BV + (j + 1) * LANE]
        dkb_o[0, :, j * LANE:(j + 1) * LANE] = kn.astype(BF)
        if kv_transposed:
            bvt = jnp.transpose(bv)
            dq_o[0, j * LANE:(j + 1) * LANE, :] = jnp.transpose(qn).astype(BF)
            dk_o[0, 0, j * LANE:(j + 1) * LANE, :] = jnp.transpose(kn)
            dv_o[0, 0, j * LANE:(j + 1) * LANE, :] = bvt
            dvb_o[0, j * LANE:(j + 1) * LANE, :] = bvt.astype(BF)
        else:
            dq_o[0, :, j * LANE:(j + 1) * LANE] = qn.astype(BF)
            dk_o[0, :, j * LANE:(j + 1) * LANE] = kn
            dv_o[0, :, j * LANE:(j + 1) * LANE] = bv
            dvb_o[0, :, j * LANE:(j + 1) * LANE] = bv.astype(BF)
    iqs = z[:, SEG_IQ:SEG_IQ + H_IDX * D_IDX] * (D_IDX ** -0.5)
    ik = z[:, SEG_IK:SEG_IK + D_IDX]
    iw = z[:, SEG_IW:SEG_IW + LANE] * (H_IDX ** -0.5)
    ik_o[0] = ik
    if kv_transposed:
        for c in range(H_IDX * D_IDX // LANE):
            iq_o[0, c * LANE:(c + 1) * LANE, :] = jnp.transpose(iqs[:, c * LANE:(c + 1) * LANE]).astype(BF)
        ikt_o[0] = ik.astype(BF)
        iw_o[0] = jnp.transpose(iw)[0:H_IDX, :]
    else:
        iq_o[0] = iqs.astype(BF)
        ikt_o[0] = _dot(ik, tile_ref[...]).astype(BF)
        iw_o[0] = iw
    u_o[0] = z[:, SEG_U:SEG_U + S5_CH]


def _proj(x, sc, sh, tabq, tabk, w, tm, kv_stack=None, layer=0):
    g, r, _ = x.shape
    per_token = sc.shape[1] != 1
    mod_spec = (pl.BlockSpec((1, tm, D_MODEL), lambda b, i: (b, i, 0)) if per_token
                else pl.BlockSpec((1, 1, D_MODEL), lambda b, i: (b, 0, 0)))
    const = lambda shape: pl.BlockSpec(shape, lambda b, i: tuple(0 for _ in shape))
    tab_spec = pl.BlockSpec((3, tm, LANE), lambda b, i: (0, i, 0))
    rows = lambda n, dt: (pl.BlockSpec((1, tm, n), lambda b, i: (b, i, 0)), jax.ShapeDtypeStruct((g, r, n), dt))
    cols = lambda n, dt: (pl.BlockSpec((1, n, tm), lambda b, i: (b, 0, i)), jax.ShapeDtypeStruct((g, n, r), dt))
    kv = 384
    nidx = H_IDX * D_IDX
    if kv_stack is None:
        dq, dk, dv, dvb = rows(kv, BF), rows(kv, F32), rows(kv, F32), rows(kv, BF)
        iq, ikt, iw = rows(nidx, BF), rows(nidx, BF), rows(LANE, F32)
    else:
        stack = (pl.BlockSpec((1, 1, kv, tm), lambda b, i: (layer, b, 0, i)),
                 jax.ShapeDtypeStruct((DEPTH, g, kv, r), F32))
        dq, dk, dv, dvb = cols(kv, BF), stack, stack, cols(kv, BF)
        iq, ikt, iw = cols(nidx, BF), rows(D_IDX, BF), cols(H_IDX, F32)
    outs = [rows(H_A * LANE, BF), rows(KV_LORA, F32), rows(ROPE_D, F32), dq, dk, rows(kv, BF), dv, dvb,
            iq, rows(D_IDX, F32), ikt, iw, rows(S5_CH, F32)]
    in_specs = [pl.BlockSpec((1, tm, D_MODEL), lambda b, i: (b, i, 0)), mod_spec, mod_spec,
                const((1, D_MODEL)), const((D_MODEL, IN_COLS_PADDED)), const((1, Q_LORA)),
                const((Q_LORA, H_A * LANE)), const((1, LANE)), tab_spec, const((1, KV_LORA)), const((1, LANE)),
                tab_spec, const((1, LANE)), const((1, LANE)), const((D_IDX, H_IDX * D_IDX))]
    args = [x, sc, sh, w["norm1_g"], w["w_in"], w["mla_gq"], w["mla_wuq"], w["mla_gq_slot"], tabq, w["mla_gkv"],
            w["mla_gkr"], tabk, w["dsa_gq"], w["dsa_gk"], w["idx_tile"]]
    aliases = {}
    if kv_stack is not None:
        aliases = {len(args): 4, len(args) + 1: 6}
        in_specs += [pl.BlockSpec(memory_space=pl.ANY), pl.BlockSpec(memory_space=pl.ANY)]
        args += list(kv_stack)
    return pl.pallas_call(
        functools.partial(_proj_body, kv_stack is not None),
        grid=(g, r // tm),
        in_specs=in_specs,
        out_specs=[o[0] for o in outs],
        out_shape=[o[1] for o in outs],
        input_output_aliases=aliases,
        compiler_params=_params(("arbitrary", "arbitrary")),
        name="in_proj",
    )(*args)


def _up_project(ckv, kr, wk_ref, wv_ref, gkn_ref, place_ref):
    krp = jnp.dot(kr, place_ref[...], precision=lax.Precision.HIGHEST, preferred_element_type=F32)
    kn = _dot(ckv, wk_ref[...])
    slots = []
    for hh in range(H_A):
        ks = kn[:, hh * LANE:(hh + 1) * LANE]
        slots.append((ks * _rs(ks, NOPE_D) * gkn_ref[...] + krp).astype(BF))
    return slots, _dot(ckv, wv_ref[...])


def _kvup_body(v_transposed, ckv_ref, kr_ref, wk_ref, wv_ref, gkn_ref, place_ref, k_o, v_o):
    slots, v = _up_project(ckv_ref[0], kr_ref[...], wk_ref, wv_ref, gkn_ref, place_ref)
    for hh in range(H_A):
        k_o[:, hh * LANE:(hh + 1) * LANE] = slots[hh]
    if v_transposed:
        for j in range(H_A * V_D // LANE):
            v_o[0, j * LANE:(j + 1) * LANE, :] = jnp.transpose(v[:, j * LANE:(j + 1) * LANE]).astype(BF)
    else:
        v_o[...] = v.astype(BF)


def _kvup(ckv, kr, w, tm, layer=0, batch_len=None):
    n = ckv.shape[1]
    const = lambda shape: pl.BlockSpec(shape, lambda i: tuple(0 for _ in shape))
    nv = H_A * V_D
    if batch_len is None:
        v_spec, v_shape = pl.BlockSpec((tm, nv), lambda i: (i, 0)), jax.ShapeDtypeStruct((n, nv), BF)
    else:
        per = batch_len // tm
        v_spec = pl.BlockSpec((1, nv, tm), lambda i: (i // per, 0, i % per))
        v_shape = jax.ShapeDtypeStruct((n // batch_len, nv, batch_len), BF)
    return pl.pallas_call(
        functools.partial(_kvup_body, batch_len is not None),
        grid=(n // tm,),
        in_specs=[pl.BlockSpec((1, tm, KV_LORA), lambda i: (layer, i, 0)), pl.BlockSpec((tm, ROPE_D), lambda i: (i, 0)),
                  const((KV_LORA, H_A * LANE)), const((KV_LORA, nv)), const((1, LANE)), const((ROPE_D, LANE))],
        out_specs=[pl.BlockSpec((tm, H_A * LANE), lambda i: (i, 0)), v_spec],
        out_shape=[jax.ShapeDtypeStruct((n, H_A * LANE), BF), v_shape],
        compiler_params=_params(("arbitrary",)),
        name="mla_kv_up",
    )(ckv, kr, w["mla_wk"], w["mla_wv"], w["mla_gkn_slot"], w["kr_place"])


def _visible(tq, sp, q0, s_valid):
    qpos = q0 + lax.broadcasted_iota(jnp.int32, (tq, sp), 0)
    kpos = lax.broadcasted_iota(jnp.int32, (tq, sp), 1)
    vis = lax.shift_right_logical(kpos, CHUNK_SHIFT) <= lax.shift_right_logical(qpos, CHUNK_SHIFT)
    if s_valid < sp:
        vis = jnp.logical_and(vis, kpos < s_valid)
    return vis, qpos, kpos


def _causal_variants(t, tq, compute):
    seg = min(CAUSAL_SEG, t)
    i = pl.program_id(1)
    for sg in range(t // seg):
        first, last = sg * seg // tq, (sg + 1) * seg // tq
        pl.when(jnp.logical_and(i >= first, i < last))(functools.partial(compute, (sg + 1) * seg))


def _mla_decode_body(p, t, q_ref, k_ref, v_ref, ckvp_ref, krp_ref, wk_ref, wv_ref, gkn_ref, place_ref, o_ref, kall, vall):
    sp = kall.shape[0]
    slots, v = _up_project(ckvp_ref[0], krp_ref[...], wk_ref, wv_ref, gkn_ref, place_ref)
    for hh in range(H_A):
        kall[0:p, hh * LANE:(hh + 1) * LANE] = slots[hh]
    vall[0:p, :] = v.astype(BF)
    kall[p:p + t, :] = k_ref[0]
    vall[p:p + t, :] = v_ref[0]
    if p + t < sp:
        kall[p + t:sp, :] = jnp.zeros((sp - p - t, kall.shape[1]), BF)
        vall[p + t:sp, :] = jnp.zeros((sp - p - t, vall.shape[1]), BF)
    _mla_compute(p, t, t, q_ref, lambda n: kall[...], lambda n: vall[...], o_ref, sp)


def _mla_compute(p, t, tq, q_ref, load_k, load_v, o_ref, svis):
    k = load_k(svis)
    v = load_v(svis)
    q = q_ref[0]
    vis, _, _ = _visible(tq, svis, p + pl.program_id(1) * tq, p + t)
    lo = lax.broadcasted_iota(jnp.int32, (tq, LANE), 1) < (LANE // 2)
    for j in range(H_A // 2):
        vpair = v[:, j * LANE:(j + 1) * LANE]
        outs = []
        for e in range(2):
            hh = 2 * j + e
            s = _dot_nt(q[:, hh * LANE:(hh + 1) * LANE], k[:, hh * LANE:(hh + 1) * LANE])
            s = jnp.where(vis, s, NEG)
            pr = jnp.exp(s - jnp.max(s, axis=-1, keepdims=True))
            outs.append(_dot(pr, vpair) / jnp.sum(pr, axis=-1, keepdims=True))
        o_ref[0, :, j * LANE:(j + 1) * LANE] = jnp.where(lo, outs[0], outs[1]).astype(BF)


def _mla_decode(q, k, v, ckv_cache, kr_past, w, layer, p):
    b, t, _ = q.shape
    sp = -(-(p + t) // LANE) * LANE
    nv = H_A * V_D
    batch = lambda n, wd: pl.BlockSpec((1, n, wd), lambda bi, i: (bi, 0, 0))
    const = lambda shape: pl.BlockSpec(shape, lambda bi, i: tuple(0 for _ in shape))
    return pl.pallas_call(
        functools.partial(_mla_decode_body, p, t),
        grid=(b, 1),
        in_specs=[batch(t, H_A * LANE), batch(t, H_A * LANE), batch(t, nv),
                  pl.BlockSpec((1, p, KV_LORA), lambda bi, i: (layer, bi, 0)),
                  pl.BlockSpec((p, ROPE_D), lambda bi, i: (bi, 0)),
                  const((KV_LORA, H_A * LANE)), const((KV_LORA, nv)), const((1, LANE)), const((ROPE_D, LANE))],
        out_specs=batch(t, nv),
        out_shape=jax.ShapeDtypeStruct((b, t, nv), BF),
        scratch_shapes=[pltpu.VMEM((sp, H_A * LANE), BF), pltpu.VMEM((sp, nv), BF)],
        compiler_params=_params(("arbitrary", "arbitrary")),
        name="mla_decode",
    )(q, k, v, ckv_cache, kr_past, w["mla_wk"], w["mla_wv"], w["mla_gkn_slot"], w["kr_place"])


def _key_to_f32(key):
    return lax.bitcast_convert_type(jnp.where(key >= 0, key, key ^ 0x7FFFFFFF), F32)


def _kth_largest(count_ge, shape, kk):
    t0 = jnp.where(count_ge(jnp.zeros(shape, F32)) >= kk, 0, INT_MIN).astype(jnp.int32)

    def bit_step(i, tk):
        cand = tk + lax.shift_left(jnp.int32(1), 30 - i)
        return jnp.where(count_ge(_key_to_f32(cand)) >= kk, cand, tk)

    tk = lax.fori_loop(0, 31, bit_step, t0)
    return _key_to_f32(jnp.maximum(tk, KEY_OF_NEG_INF))


def _kth_largest_two_phase(count_ge_bf16, count_ge, shape, kk):
    def grid_f32(k16):
        bits16 = jnp.where(k16 >= 0, k16, k16 ^ 0x7FFF)
        return lax.bitcast_convert_type(lax.shift_left(bits16, 16), F32)

    t16 = jnp.where(count_ge_bf16(jnp.zeros(shape, F32)) >= kk, 0, -32768).astype(jnp.int32)

    def coarse_step(i, tk):
        cand = tk + lax.shift_left(jnp.int32(1), 14 - i)
        return jnp.where(count_ge_bf16(grid_f32(cand)) >= kk, cand, tk)

    t16 = lax.fori_loop(0, 15, coarse_step, t16)
    base = lax.shift_left(t16, 16) + jnp.where(t16 < 0, 0xFFFF, 0) - 0x8000

    def fine_step(i, off):
        cand = off + lax.shift_left(jnp.int32(1), 16 - i)
        return jnp.where(count_ge(_key_to_f32(base + cand)) >= kk, cand, off)

    off = lax.fori_loop(0, 17, fine_step, jnp.zeros(shape, jnp.int32))
    return _key_to_f32(jnp.maximum(base + off, KEY_OF_NEG_INF))


def _tie_index_limit(count, count_tied_below, thr, kk, n_keys):
    n_gt = count(lambda x: x > thr)
    need = kk - n_gt
    excess = (count(lambda x: x >= thr) - n_gt) > need

    def bisect():
        def step(i, lim):
            cand = lim + lax.shift_left(jnp.int32(1), (n_keys - 1).bit_length() - 1 - i)
            return jnp.where(count_tied_below(cand) < need, cand, lim)

        lim = lax.fori_loop(0, (n_keys - 1).bit_length(), step, jnp.zeros(thr.shape, jnp.int32))
        return jnp.where(excess, lim, NO_TIE_LIMIT)

    any_excess = jnp.max(jnp.where(excess, 1.0, 0.0)) > 0.5
    return lax.cond(any_excess, bisect, lambda: jnp.full(thr.shape, NO_TIE_LIMIT, jnp.int32))


def _visible_block(kb, q0, tq, svis):
    if (kb + 1) * KEY_BLOCK <= svis - CAUSAL_SEG:
        return None
    kpos = kb * KEY_BLOCK + lax.broadcasted_iota(jnp.int32, (KEY_BLOCK, tq), 0)
    qpos = q0 + lax.broadcasted_iota(jnp.int32, (KEY_BLOCK, tq), 1)
    return lax.shift_right_logical(kpos, CHUNK_SHIFT) <= lax.shift_right_logical(qpos, CHUNK_SHIFT)


def _block_max(m, s):
    return jnp.maximum(m, jnp.max(s.reshape(KEY_BLOCK // 8, 8, s.shape[1]), axis=0))


def _softmax_pv(n_heads, m_acc, s_ref, p_ref, vt_ref, o_ref, svis):
    tq = p_ref.shape[1]
    nkb = svis // KEY_BLOCK
    for j in range(n_heads // 2):
        halves = []
        for e in range(2):
            hh = 2 * j + e
            m = jnp.max(m_acc[hh], axis=0, keepdims=True)
            l_acc = jnp.zeros((4, 8, tq), F32)
            for kb in range(nkb):
                rows = slice(kb * KEY_BLOCK, (kb + 1) * KEY_BLOCK)
                pr = jnp.exp(s_ref[hh, rows, :] - m)
                l_acc = l_acc + jnp.sum(pr.reshape(4, KEY_BLOCK // 32, 8, tq), axis=1)
                p_ref[rows, :] = pr.astype(BF)
            out_t = jnp.dot(vt_ref[0, j * LANE:(j + 1) * LANE, 0:svis], p_ref[0:svis, :], preferred_element_type=F32)
            out_t = out_t / jnp.sum(jnp.sum(l_acc, axis=0), axis=0, keepdims=True)
            halves.append(out_t[e * (LANE // 2):(e + 1) * (LANE // 2), :])
        o_ref[0, :, j * LANE:(j + 1) * LANE] = jnp.transpose(jnp.concatenate(halves, axis=0)).astype(BF)


def _mla_t_body(t, tq, q_ref, k_ref, vt_ref, o_ref, s_ref, p_ref):
    _causal_variants(t, tq, functools.partial(_mla_t_compute, tq, q_ref, k_ref, vt_ref, o_ref, s_ref, p_ref))


def _mla_t_compute(tq, q_ref, k_ref, vt_ref, o_ref, s_ref, p_ref, svis):
    q0 = pl.program_id(1) * tq
    q = q_ref[0]
    m_acc = [jnp.full((8, tq), NEG, F32) for _ in range(H_A)]
    for kb in range(svis // KEY_BLOCK):
        rows = slice(kb * KEY_BLOCK, (kb + 1) * KEY_BLOCK)
        vis = _visible_block(kb, q0, tq, svis)
        for hh in range(H_A):
            s = _dot_nt(k_ref[0, rows, hh * LANE:(hh + 1) * LANE], q[:, hh * LANE:(hh + 1) * LANE])
            if vis is not None:
                s = jnp.where(vis, s, NEG)
            s_ref[hh, rows, :] = s
            m_acc[hh] = _block_max(m_acc[hh], s)
    _softmax_pv(H_A, m_acc, s_ref, p_ref, vt_ref, o_ref, svis)


def _mla_prompt(q, k, vt, tq):
    b, t, _ = q.shape
    return pl.pallas_call(
        functools.partial(_mla_t_body, t, tq),
        grid=(b, t // tq),
        in_specs=[pl.BlockSpec((1, tq, H_A * LANE), lambda bi, i: (bi, i, 0)),
                  pl.BlockSpec((1, t, H_A * LANE), lambda bi, i: (bi, 0, 0)),
                  pl.BlockSpec((1, H_A * V_D, t), lambda bi, i: (bi, 0, 0))],
        out_specs=pl.BlockSpec((1, tq, H_A * V_D), lambda bi, i: (bi, i, 0)),
        out_shape=jax.ShapeDtypeStruct((b, t, H_A * V_D), BF),
        scratch_shapes=[pltpu.VMEM((H_A, t, tq), F32), pltpu.VMEM((t, tq), BF)],
        compiler_params=_params(("arbitrary", "arbitrary")),
        name="mla_prompt",
    )(q, k, vt)


def _dsa_t_body(t, tq, top, qt_ref, iqt_ref, iwt_ref, k_ref, vt_ref, ik_ref, o_ref, sc_ref, sc16_ref, s_ref, p_ref):
    _causal_variants(t, tq, functools.partial(_dsa_t_compute, tq, top, qt_ref, iqt_ref, iwt_ref, k_ref, vt_ref, ik_ref,
                                              o_ref, sc_ref, sc16_ref, s_ref, p_ref))


def _dsa_t_compute(tq, top, qt_ref, iqt_ref, iwt_ref, k_ref, vt_ref, ik_ref, o_ref, sc_ref, sc16_ref, s_ref, p_ref,
                   svis):
    q0 = pl.program_id(1) * tq
    nkb = svis // KEY_BLOCK

    iqt = iqt_ref[0]
    iwt = iwt_ref[0]
    for kb in range(nkb):
        rows = slice(kb * KEY_BLOCK, (kb + 1) * KEY_BLOCK)
        ikb = ik_ref[0, rows, :]
        acc = jnp.zeros((KEY_BLOCK, tq), F32)
        for hh in range(H_IDX):
            dots = jnp.dot(ikb, iqt[hh * D_IDX:(hh + 1) * D_IDX, :], preferred_element_type=F32)
            acc = acc + iwt[hh:hh + 1, :] * jnp.maximum(dots, 0.0)
        vis = _visible_block(kb, q0, tq, svis)
        if vis is not None:
            acc = jnp.where(vis, acc, -jnp.inf)
        sc_ref[rows, :] = acc
        sc16_ref[rows, :] = acc.astype(BF)

    def count(pred):
        x = sc_ref[0:svis, :].reshape(8, svis // 64, 8, tq)
        part = jnp.sum(jnp.where(pred(x), 1.0, 0.0), axis=1)
        return jnp.sum(jnp.sum(part, axis=0), axis=0, keepdims=True)

    def count_ge(cand):
        return count(lambda x: x >= cand)

    def count_ge_bf16(cand):
        x = sc16_ref[0:svis, :].reshape(8, svis // 128, 16, tq)
        c = cand.astype(BF)
        one, zero = jnp.ones((8, 16, tq), BF), jnp.zeros((8, 16, tq), BF)
        part = zero
        for i in range(svis // 128):
            part = part + jnp.where(x[:, i] >= c, one, zero)
        return jnp.sum(jnp.sum(part.astype(F32), axis=0), axis=0, keepdims=True)

    thr = _kth_largest_two_phase(count_ge_bf16, count_ge, (1, tq), float(top))

    def count_tied_below(limit):
        acc = jnp.zeros((8, tq), F32)
        for kb in range(nkb):
            x = sc_ref[kb * KEY_BLOCK:(kb + 1) * KEY_BLOCK, :]
            kpos = kb * KEY_BLOCK + lax.broadcasted_iota(jnp.int32, (KEY_BLOCK, tq), 0)
            hit = jnp.where(x == thr, jnp.where(kpos < limit, 1.0, 0.0), 0.0)
            acc = acc + jnp.sum(hit.reshape(KEY_BLOCK // 8, 8, tq), axis=0)
        return jnp.sum(acc, axis=0, keepdims=True)

    tie_limit = _tie_index_limit(count, count_tied_below, thr, float(top), svis)

    qt = qt_ref[0]
    zeros = jnp.zeros((HD_B, tq), BF)
    qms = []
    for hh in range(H_B):
        qh = qt[hh * HD_B:(hh + 1) * HD_B, :]
        qms.append(jnp.concatenate([qh, zeros] if hh % 2 == 0 else [zeros, qh], axis=0))
    m_acc = [jnp.full((8, tq), NEG, F32) for _ in range(H_B)]
    qpos = q0 + lax.broadcasted_iota(jnp.int32, (KEY_BLOCK, tq), 1)
    for kb in range(nkb):
        rows = slice(kb * KEY_BLOCK, (kb + 1) * KEY_BLOCK)
        kpos = kb * KEY_BLOCK + lax.broadcasted_iota(jnp.int32, (KEY_BLOCK, tq), 0)
        x = sc_ref[rows, :]
        bias = jnp.where(x > thr, 0.0, jnp.where(x == thr, jnp.where(kpos <= tie_limit, 0.0, NEG), NEG))
        vis = _visible_block(kb, q0, tq, svis)
        if vis is not None:
            bias = jnp.where(vis, bias, NEG)
        dist = jnp.abs(qpos - kpos).astype(F32)
        for hh in range(H_B):
            kpair = k_ref[0, rows, (hh // 2) * LANE:(hh // 2 + 1) * LANE]
            s = jnp.dot(kpair, qms[hh], preferred_element_type=F32) - (2.0 ** (-8.0 * (hh + 1) / H_B)) * dist + bias
            s_ref[hh, rows, :] = s
            m_acc[hh] = _block_max(m_acc[hh], s)
    _softmax_pv(H_B, m_acc, s_ref, p_ref, vt_ref, o_ref, svis)


def _dsa_prompt(qt, iqt, iwt, k, vt, ik, tq):
    b, _, t = qt.shape
    top = min(TOPK_MAX, t // 4)
    qcols = lambda n: pl.BlockSpec((1, n, tq), lambda bi, i: (bi, 0, i))
    batch = lambda n, wd: pl.BlockSpec((1, n, wd), lambda bi, i: (bi, 0, 0))
    return pl.pallas_call(
        functools.partial(_dsa_t_body, t, tq, top),
        grid=(b, t // tq),
        in_specs=[qcols(384), qcols(H_IDX * D_IDX), qcols(H_IDX), batch(t, 384), batch(384, t), batch(t, D_IDX)],
        out_specs=pl.BlockSpec((1, tq, 384), lambda bi, i: (bi, i, 0)),
        out_shape=jax.ShapeDtypeStruct((b, t, 384), BF),
        scratch_shapes=[pltpu.VMEM((t, tq), F32), pltpu.VMEM((t, tq), BF), pltpu.VMEM((H_B, t, tq), F32),
                        pltpu.VMEM((t, tq), BF)],
        compiler_params=_params(("arbitrary", "arbitrary")),
        name="dsa_attn",
    )(qt, iqt, iwt, k, vt, ik)


def _dsa_s_body(p, t, top, q_ref, iq_ref, iw_ref, k_ref, v_ref, ikt_ref, kpt_ref, vpt_ref, ikpt_ref, o_ref, sc_ref):
    nb = q_ref.shape[0]
    sp = sc_ref.shape[1]
    pad_rows = lambda x: jnp.concatenate([x, jnp.zeros((LANE - t, x.shape[1]), x.dtype)], axis=0)
    lane_b = lax.broadcasted_iota(jnp.int32, (t, sp), 1)
    qrow_b = p + lax.broadcasted_iota(jnp.int32, (t, sp), 0)
    valid_b = jnp.logical_and(
        lane_b < p + t, lax.shift_right_logical(lane_b, CHUNK_SHIFT) <= lax.shift_right_logical(qrow_b, CHUNK_SHIFT))
    lane = lax.broadcasted_iota(jnp.int32, (nb * t, sp), 1)
    valid = jnp.concatenate([valid_b] * nb, axis=0)

    head_of_lane = lax.shift_right_logical(lax.broadcasted_iota(jnp.int32, (t, H_IDX * D_IDX), 1), 5)
    for bb in range(nb):
        iq = iq_ref[bb]
        iw = iw_ref[bb]
        ikp = ikpt_ref[0, bb].astype(BF)
        ikp_tiled = jnp.concatenate([ikp] * H_IDX, axis=0)
        ikn = pad_rows(ikt_ref[bb])
        sc_p = jnp.zeros((t, p), F32)
        sc_n = jnp.zeros((t, LANE), F32)
        for hh in range(H_IDX):
            qh = jnp.where(head_of_lane == hh, iq, jnp.zeros_like(iq))
            wh = iw[:, hh:hh + 1]
            sc_p = sc_p + wh * jnp.maximum(jnp.dot(qh, ikp_tiled, preferred_element_type=F32), 0.0)
            sc_n = sc_n + wh * jnp.maximum(_dot_nt(qh, ikn), 0.0)
        sc_ref[bb * t:(bb + 1) * t, :] = jnp.where(valid_b, jnp.concatenate([sc_p, sc_n], axis=1), -jnp.inf)

    def count(pred):
        return jnp.sum(jnp.where(pred(sc_ref[...]), 1.0, 0.0), axis=1, keepdims=True)

    thr = _kth_largest(lambda cand: count(lambda x: x >= cand), (nb * t, 1), float(top))

    def count_tied_below(limit):
        hit = jnp.where(sc_ref[...] == thr, jnp.where(lane < limit, 1.0, 0.0), 0.0)
        return jnp.sum(hit, axis=1, keepdims=True)

    tie_limit = _tie_index_limit(count, count_tied_below, thr, float(top), sp)
    x = sc_ref[...]
    bias_all = jnp.where(x > thr, 0.0, jnp.where(x == thr, jnp.where(lane <= tie_limit, 0.0, NEG), NEG))
    bias_all = jnp.where(valid, bias_all, NEG)
    dist = jnp.abs(qrow_b - lane_b).astype(F32)

    lo = lax.broadcasted_iota(jnp.int32, (t, LANE), 1) < (LANE // 2)
    for bb in range(nb):
        q = q_ref[bb]
        bias = bias_all[bb * t:(bb + 1) * t, :]
        for j in range(H_B // 2):
            cols = slice(j * LANE, (j + 1) * LANE)
            qpair = q[:, cols]
            kpt = kpt_ref[0, bb, cols, :].astype(BF)
            vpt = vpt_ref[0, bb, cols, :].astype(BF)
            kn = pad_rows(k_ref[bb][:, cols])
            vn = pad_rows(v_ref[bb][:, cols])
            outs = []
            for e in range(2):
                hh = 2 * j + e
                qh = jnp.where(lo if e == 0 else jnp.logical_not(lo), qpair, jnp.zeros_like(qpair))
                s = jnp.concatenate([jnp.dot(qh, kpt, preferred_element_type=F32), _dot_nt(qh, kn)], axis=1)
                s = s - (2.0 ** (-8.0 * (hh + 1) / H_B)) * dist + bias
                pr = jnp.exp(s - jnp.max(s, axis=-1, keepdims=True))
                prb = pr.astype(BF)
                o = _dot_nt(prb[:, 0:p], vpt) + jnp.dot(prb[:, p:sp], vn, preferred_element_type=F32)
                outs.append(o / jnp.sum(pr, axis=-1, keepdims=True))
            o_ref[bb, :, cols] = jnp.where(lo, outs[0], outs[1]).astype(BF)


def _dsa_decode(q, iq, iw, k, v, ikt, kpt, vpt, ikpt, layer):
    b, t, _ = q.shape
    p = kpt.shape[-1]
    sp = p + LANE
    top = min(TOPK_MAX, (p + t) // 4)
    nb = math.gcd(b, DECODE_BATCHES_PER_STEP)
    new = lambda wd: pl.BlockSpec((nb, t, wd), lambda bi: (bi, 0, 0))
    cache = lambda n: pl.BlockSpec((1, nb, n, p), lambda bi: (layer, bi, 0, 0))
    return pl.pallas_call(
        functools.partial(_dsa_s_body, p, t, top),
        grid=(b // nb,),
        in_specs=[new(384), new(256), new(LANE), new(384), new(384), new(256), cache(384), cache(384), cache(D_IDX)],
        out_specs=new(384),
        out_shape=jax.ShapeDtypeStruct((b, t, 384), BF),
        scratch_shapes=[pltpu.VMEM((nb * t, sp), F32)],
        compiler_params=_params(("arbitrary",)),
        name="dsa_decode",
    )(q, iq, iw, k, v, ikt, kpt, vpt, ikpt)


def _s5_body(has_h0, nb, tt, u_ref, are_ref, aim_ref, ldt_ref, bre_ref, bim_ref, cre_ref, cim_ref, d_ref, wg_ref,
             bg_ref, *rest):
    if has_h0:
        h0re_ref, h0im_ref, o_ref, sre_o, sim_o, xre, xim = rest
    else:
        o_ref, sre_o, sim_o, xre, xim = rest

    @pl.when(pl.program_id(0) == 0)
    def _():
        if has_h0:
            sre_o[...] = h0re_ref[...]
            sim_o[...] = h0im_ref[...]
        else:
            sre_o[...] = jnp.zeros_like(sre_o)
            sim_o[...] = jnp.zeros_like(sim_o)

    ar = are_ref[...]
    ai = aim_ref[...]
    dt = jnp.exp(ldt_ref[...])
    mag = jnp.exp(dt * ar)
    ab_re = mag * jnp.cos(dt * ai)
    ab_im = mag * jnp.sin(dt * ai)
    den = ar * ar + ai * ai
    nr = ab_re - 1.0
    f_re = (nr * ar + ab_im * ai) / den
    f_im = (ab_im * ar - nr * ai) / den

    u = u_ref[...].reshape(tt * nb, S5_CH)
    bu_re = _dot(u, bre_ref[...])
    bu_im = _dot(u, bim_ref[...])
    xre[...] = f_re * bu_re - f_im * bu_im
    xim[...] = f_re * bu_im + f_im * bu_re

    a_re = jnp.broadcast_to(ab_re, (nb, S5_N))
    a_im = jnp.broadcast_to(ab_im, (nb, S5_N))

    def step(ti, carry):
        s_re, s_im = carry
        rows = pl.ds(pl.multiple_of(ti * nb, nb), nb)
        n_re = a_re * s_re - a_im * s_im + xre[rows, :]
        n_im = a_re * s_im + a_im * s_re + xim[rows, :]
        xre[rows, :] = n_re
        xim[rows, :] = n_im
        return n_re, n_im

    s_re, s_im = lax.fori_loop(0, tt, step, (sre_o[...], sim_o[...]), unroll=2)
    sre_o[...] = s_re
    sim_o[...] = s_im

    y = _dot(xre[...], cre_ref[...]) - _dot(xim[...], cim_ref[...]) + d_ref[...] * u
    g = _dot(y, wg_ref[...]) + bg_ref[...]
    out = g[:, 0:S5_CH] * (1.0 / (1.0 + jnp.exp(-g[:, S5_CH:2 * S5_CH])))
    o_ref[...] = out.reshape(tt, nb, S5_CH)


def _s5(u, h0, w, tt):
    nb, t, _ = u.shape
    u = jnp.swapaxes(u, 0, 1)
    const = lambda shape: pl.BlockSpec(shape, lambda i: tuple(0 for _ in shape))
    in_specs = [pl.BlockSpec((tt, nb, S5_CH), lambda i: (i, 0, 0)), const((1, S5_N)), const((1, S5_N)),
                const((1, S5_N)), const((S5_CH, S5_N)), const((S5_CH, S5_N)), const((S5_N, S5_CH)),
                const((S5_N, S5_CH)), const((1, S5_CH)), const((S5_CH, 2 * S5_CH)), const((1, 2 * S5_CH))]
    args = [u, w["s5_a_re"], w["s5_a_im"], w["s5_log_dt"], w["s5_bre"], w["s5_bim"], w["s5_cre"], w["s5_cim"],
            w["s5_d"], w["s5_w_glu"], w["s5_b_glu"]]
    if h0 is not None:
        in_specs += [const((nb, S5_N)), const((nb, S5_N))]
        args += [h0[0], h0[1]]
    out, s_re, s_im = pl.pallas_call(
        functools.partial(_s5_body, h0 is not None, nb, tt),
        grid=(t // tt,),
        in_specs=in_specs,
        out_specs=[pl.BlockSpec((tt, nb, S5_CH), lambda i: (i, 0, 0)), const((nb, S5_N)), const((nb, S5_N))],
        out_shape=[jax.ShapeDtypeStruct((t, nb, S5_CH), F32), jax.ShapeDtypeStruct((nb, S5_N), F32),
                   jax.ShapeDtypeStruct((nb, S5_N), F32)],
        scratch_shapes=[pltpu.VMEM((tt * nb, S5_N), F32), pltpu.VMEM((tt * nb, S5_N), F32)],
        compiler_params=_params(("arbitrary",)),
        name="s5_scan",
    )(*args)
    return jnp.swapaxes(out, 0, 1).astype(BF), s_re, s_im


def _out_body(a_ref, b_ref, c_ref, x_ref, g1_ref, sc_ref, sh_ref, g2_ref, gn_ref, wa_ref, wb_ref, wc_ref, w1_ref,
              w2_ref, o_ref):
    mix = _dot(a_ref[0], wa_ref[...]) + _dot(b_ref[0], wb_ref[...]) + _dot(c_ref[0], wc_ref[...])
    x1 = x_ref[0] + g1_ref[0] * mix
    h2 = (x1 * _rs(x1, D_MODEL) * gn_ref[...] * (1.0 + sc_ref[0]) + sh_ref[0]).astype(BF)
    ff = jnp.zeros_like(x1)
    for c in range(D_FF // D_MODEL):
        hid = jnp.maximum(_dot(h2, w1_ref[:, c * D_MODEL:(c + 1) * D_MODEL]), 0.0)
        ff = ff + _dot(hid * hid, w2_ref[c * D_MODEL:(c + 1) * D_MODEL, :])
    o_ref[0] = x1 + g2_ref[0] * ff


def _out(a, b, c, x, g1, sc, sh, g2, w, tm):
    g, r, _ = x.shape
    per_token = sc.shape[1] != 1
    mod_spec = (pl.BlockSpec((1, tm, D_MODEL), lambda bi, i: (bi, i, 0)) if per_token
                else pl.BlockSpec((1, 1, D_MODEL), lambda bi, i: (bi, 0, 0)))
    tile = lambda n: pl.BlockSpec((1, tm, n), lambda bi, i: (bi, i, 0))
    const = lambda shape: pl.BlockSpec(shape, lambda bi, i: tuple(0 for _ in shape), pipeline_mode=pl.Buffered(1))
    return pl.pallas_call(
        _out_body,
        grid=(g, r // tm),
        in_specs=[tile(384), tile(384), tile(S5_CH), tile(D_MODEL), mod_spec, mod_spec, mod_spec, mod_spec,
                  const((1, D_MODEL)), const((384, D_MODEL)), const((384, D_MODEL)), const((S5_CH, D_MODEL)),
                  const((D_MODEL, D_FF)), const((D_FF, D_MODEL))],
        out_specs=tile(D_MODEL),
        out_shape=jax.ShapeDtypeStruct((g, r, D_MODEL), F32),
        compiler_params=_params(("arbitrary", "arbitrary")),
        name="out_mlp",
    )(a, b, c, x, g1, sc, sh, g2, w["norm2_g"], w["w_out_a"], w["w_out_b"], w["w_out_c"], w["ff_w1"], w["ff_w2"])


def _slot_pad(v, lo, n):
    return jnp.pad(v, ((0, 0), (lo, LANE - lo - n)))


def _place_cols(wm, segments, total):
    parts, at = [], 0
    for dst, src, n in segments:
        if dst > at:
            parts.append(jnp.zeros(wm.shape[:-1] + (dst - at,), wm.dtype))
        parts.append(wm[..., src:src + n])
        at = dst + n
    if total > at:
        parts.append(jnp.zeros(wm.shape[:-1] + (total - at,), wm.dtype))
    return jnp.concatenate(parts, axis=-1)


def _prep_weights(p):
    f = {}
    offs = np.cumsum((0,) + IN_SIZES)
    starts = (SEG_CQ, SEG_CKV, SEG_KR, SEG_BQ, SEG_BK, SEG_BV, SEG_IQ, SEG_IK, SEG_IW, SEG_U)
    f["w_in"] = _place_cols(p["w_in"], [(s, int(o), n) for s, o, n in zip(starts, offs[:-1], IN_SIZES)],
                            IN_COLS_PADDED).astype(BF)
    kvw = NOPE_D + V_D
    f["mla_wuq"] = _place_cols(p["mla_wuq"], [(hh * LANE, hh * QK_D, QK_D) for hh in range(H_A)], H_A * LANE).astype(BF)
    f["mla_wk"] = _place_cols(p["mla_wukv"], [(hh * LANE, hh * kvw, NOPE_D) for hh in range(H_A)], H_A * LANE).astype(BF)
    f["mla_wv"] = _place_cols(p["mla_wukv"], [(hh * V_D, hh * kvw + NOPE_D, V_D) for hh in range(H_A)],
                              H_A * V_D).astype(BF)
    f["mla_gq_slot"] = jnp.concatenate([p["mla_gqn"], p["mla_gqr"], jnp.zeros((DEPTH, LANE - QK_D), F32)], axis=-1)
    f["mla_gkn_slot"] = _slot_pad(p["mla_gkn"], 0, NOPE_D)
    f["mla_gkr"] = _slot_pad(p["mla_gkr"], 0, ROPE_D)
    f["dsa_gq"] = jnp.concatenate([p["dsa_gq"], p["dsa_gq"]], axis=-1)
    f["dsa_gk"] = jnp.concatenate([p["dsa_gk"], p["dsa_gk"]], axis=-1)
    for name in ("norm1_g", "norm2_g", "mla_gq", "mla_gkv", "s5_d", "s5_b_glu"):
        f[name] = p[name]
    f["w_out_a"] = p["w_out"][:, 0:384].astype(BF)
    f["w_out_b"] = p["w_out"][:, 384:768].astype(BF)
    f["w_out_c"] = p["w_out"][:, 768:1024].astype(BF)
    f["ff_w1"] = p["ff_w1"].astype(BF)
    f["ff_w2"] = p["ff_w2"].astype(BF)
    f["s5_w_glu"] = p["s5_w_glu"].astype(BF)
    eye = jnp.eye(S5_G, dtype=F32)
    f["s5_bre"] = jnp.einsum("lgpc,gh->lgchp", p["s5_b_re"], eye).reshape(DEPTH, S5_CH, S5_N).astype(BF)
    f["s5_bim"] = jnp.einsum("lgpc,gh->lgchp", p["s5_b_im"], eye).reshape(DEPTH, S5_CH, S5_N).astype(BF)
    f["s5_cre"] = jnp.einsum("lgcp,gh->lgphc", p["s5_c_re"], eye).reshape(DEPTH, S5_N, S5_CH).astype(BF)
    f["s5_cim"] = jnp.einsum("lgcp,gh->lgphc", p["s5_c_im"], eye).reshape(DEPTH, S5_N, S5_CH).astype(BF)
    f["s5_a_re"] = p["s5_a_re"].reshape(DEPTH, S5_N)
    f["s5_a_im"] = p["s5_a_im"].reshape(DEPTH, S5_N)
    f["s5_log_dt"] = jnp.repeat(p["s5_log_dt"], S5_P, axis=-1)
    return f


def _layer_weights(f, l):
    w = {k: (v[l] if v.ndim == 3 else v[l][None, :]) for k, v in f.items()}
    place = np.zeros((ROPE_D, LANE), np.float32)
    place[np.arange(ROPE_D), NOPE_D + np.arange(ROPE_D)] = 1.0
    w["kr_place"] = jnp.asarray(place)
    tile = np.zeros((D_IDX, H_IDX * D_IDX), np.float32)
    for hh in range(H_IDX):
        tile[np.arange(D_IDX), hh * D_IDX + np.arange(D_IDX)] = 1.0
    w["idx_tile"] = jnp.asarray(tile, dtype=BF)
    return w


def _rope_tables(pos, lo):
    half = ROPE_D // 2
    inv = ROPE_BASE ** (-jnp.arange(half, dtype=F32) / half)
    ang = pos.astype(F32)[:, None] * inv[None, :]
    cos, sin = jnp.cos(ang), jnp.sin(ang)
    n = pos.shape[0]
    z = lambda w_: jnp.zeros((n, w_), F32)
    tail = LANE - lo - ROPE_D
    c = jnp.concatenate([jnp.ones((n, lo), F32), cos, cos, z(tail)], axis=-1)
    s1 = jnp.concatenate([z(lo), -sin, z(half), z(tail)], axis=-1)
    s2 = jnp.concatenate([z(lo), z(half), sin, z(tail)], axis=-1)
    return jnp.stack([c, s1, s2])


def _prompt_layer(x, mods, w, pos, tiles, layer, kv_stack):
    b, t, _ = x.shape
    tm, tq_mla, tq_dsa, tt = tiles
    sh1, sc1, g1, sh2, sc2, g2 = mods
    outs = _proj(x, sc1, sh1, _rope_tables(pos, NOPE_D), _rope_tables(pos, 0), w, tm, kv_stack, layer)
    q, ckv, kr, dqt, dk_stack, dkb, dv_stack, dvtb, iqt, ik, ikb, iwt, u = outs
    kn, vnt = _kvup(ckv.reshape(1, b * t, KV_LORA), kr.reshape(b * t, ROPE_D), w, min(512, t), batch_len=t)
    a_out = _mla_prompt(q, kn.reshape(b, t, H_A * LANE), vnt, tq_mla)
    b_out = _dsa_prompt(dqt, iqt, iwt, dkb, dvtb, ikb, tq_dsa)
    c_out, s_re, s_im = _s5(u, None, w, tt)
    y = _out(a_out, b_out, c_out, x, g1, sc2, sh2, g2, w, tm)
    state = (ckv, kr, None, None, ik, s_re.reshape(b, S5_G, S5_P), s_im.reshape(b, S5_G, S5_P))
    return y, state, (dk_stack, dv_stack)


def _decode_layer(x, mods, caches, w, pos, layer):
    b, t, _ = x.shape
    n = b * t
    c_ckv, c_kr, c_kt, c_vt, c_ikt, c_sre, c_sim = caches
    plen = c_kt.shape[-1]
    xf = x.reshape(1, n, D_MODEL)
    sh1, sc1, g1, sh2, sc2, g2 = (jnp.broadcast_to(m, (b, t, D_MODEL)).reshape(1, n, D_MODEL) for m in mods)
    tabq = jnp.tile(_rope_tables(pos, NOPE_D), (1, b, 1))
    tabk = jnp.tile(_rope_tables(pos, 0), (1, b, 1))
    outs = _proj(xf, sc1, sh1, tabq, tabk, w, n)
    q, ckv, kr, dq, dk, dkb, dv, dvb, iq, ik, ikt, iw, u = (o.reshape(b, t, o.shape[-1]) for o in outs)
    kn, vn = _kvup(ckv.reshape(1, n, KV_LORA), kr.reshape(n, ROPE_D), w, n)
    a_out = _mla_decode(q, kn.reshape(b, t, H_A * LANE), vn.reshape(b, t, H_A * V_D), c_ckv,
                        c_kr[layer].reshape(b * plen, ROPE_D), w, layer, plen)
    b_out = _dsa_decode(dq, iq, iw, dkb, dvb, ikt, c_kt, c_vt, c_ikt, layer)
    c_out, s_re, s_im = _s5(u, (c_sre[layer], c_sim[layer]), w, t)
    y = _out(a_out.reshape(1, n, -1), b_out.reshape(1, n, -1), c_out.reshape(1, n, -1), xf, g1, sc2, sh2, g2, w, n)
    state = (ckv, kr, dk.reshape(b, t, H_B, HD_B), dv.reshape(b, t, H_B, HD_B), ik,
             s_re.reshape(b, S5_G, S5_P), s_im.reshape(b, S5_G, S5_P))
    return y.reshape(b, t, D_MODEL), state


def kernel(x_prompt, x_sample, c_prompt, c_sample, cache_mla_ckv, cache_mla_krope, cache_dsa_k, cache_dsa_v, cache_dsa_idxk, state_s5_re, state_s5_im, ada_w, ada_b, norm1_g, norm2_g, w_in, w_out, mla_gq, mla_wuq, mla_gkv, mla_wukv, mla_gqn, mla_gqr, mla_gkn, mla_gkr, dsa_gq, dsa_gk, s5_a_re, s5_a_im, s5_b_re, s5_b_im, s5_c_re, s5_c_im, s5_d, s5_log_dt, s5_w_glu, s5_b_glu, ff_w1, ff_w2):
    params = dict(norm1_g=norm1_g, norm2_g=norm2_g, w_in=w_in, w_out=w_out, mla_gq=mla_gq, mla_wuq=mla_wuq,
                  mla_gkv=mla_gkv, mla_wukv=mla_wukv, mla_gqn=mla_gqn, mla_gqr=mla_gqr, mla_gkn=mla_gkn,
                  mla_gkr=mla_gkr, dsa_gq=dsa_gq, dsa_gk=dsa_gk, s5_a_re=s5_a_re, s5_a_im=s5_a_im, s5_b_re=s5_b_re,
                  s5_b_im=s5_b_im, s5_c_re=s5_c_re, s5_c_im=s5_c_im, s5_d=s5_d, s5_log_dt=s5_log_dt,
                  s5_w_glu=s5_w_glu, s5_b_glu=s5_b_glu, ff_w1=ff_w1, ff_w2=ff_w2)
    f = _prep_weights(params)
    nbp, tp, _ = x_prompt.shape
    nbs, ts, _ = x_sample.shape
    past_len = cache_mla_ckv.shape[2]
    mod = _ada(jnp.concatenate([c_prompt, c_sample], axis=0), ada_w, ada_b)
    pos_p = jnp.arange(tp, dtype=jnp.int32)
    pos_s = past_len + jnp.arange(ts, dtype=jnp.int32)
    caches = (cache_mla_ckv.reshape(DEPTH, nbs * past_len, KV_LORA), cache_mla_krope,
              jnp.transpose(cache_dsa_k, (0, 1, 3, 4, 2)).reshape(DEPTH, nbs, H_B * HD_B, past_len),
              jnp.transpose(cache_dsa_v, (0, 1, 3, 4, 2)).reshape(DEPTH, nbs, H_B * HD_B, past_len),
              jnp.transpose(cache_dsa_idxk, (0, 1, 3, 2)),
              state_s5_re.reshape(DEPTH, nbs, S5_N), state_s5_im.reshape(DEPTH, nbs, S5_N))
    xp, xs = x_prompt, x_sample
    new_p = [[] for _ in range(7)]
    new_s = [[] for _ in range(7)]
    kv_stack = tuple(jnp.zeros((DEPTH, nbp, H_B * HD_B, tp), F32) for _ in range(2))
    for l in range(DEPTH):
        w = _layer_weights(f, l)
        mods = [mod[l][:, None, i * D_MODEL:(i + 1) * D_MODEL] for i in range(6)]
        xp, st_p, kv_stack = _prompt_layer(xp, [m[:nbp] for m in mods], w, pos_p,
                                           (min(512, tp), min(128, tp), min(128, tp), min(128, tp)), l, kv_stack)
        xs, st_s = _decode_layer(xs, [m[nbp:] for m in mods], caches, w, pos_s, l)
        for i in range(7):
            new_p[i].append(st_p[i])
            new_s[i].append(st_s[i])
    outs = [xp, xs]
    for i in range(7):
        if i in (2, 3):
            stacked = jnp.transpose(kv_stack[i - 2].reshape(DEPTH, nbp, H_B, HD_B, tp), (0, 1, 4, 2, 3))
        else:
            stacked = jnp.stack(new_p[i])
        outs.append(stacked)
        outs.append(jnp.stack(new_s[i]))
    return tuple(outs)
```

```python
import functools
import math

import numpy as np
import jax
import jax.numpy as jnp
from jax import lax
from jax.experimental import pallas as pl
from jax.experimental.pallas import tpu as pltpu

D_MODEL = 1024
DEPTH = 4
CHUNK_SHIFT = 6
EPS = 1e-6
D_FF = 4 * D_MODEL
H_A = 6
NOPE_D = 64
ROPE_D = 32
V_D = 64
QK_D = NOPE_D + ROPE_D
Q_LORA = 256
KV_LORA = 128
ROPE_BASE = 10000.0
H_B = 6
HD_B = 64
H_IDX = 8
D_IDX = 32
TOPK_MAX = 256
S5_CH = D_MODEL // 4
S5_GROUP = 16
S5_G = S5_CH // S5_GROUP
S5_P = 64
S5_N = S5_G * S5_P
IN_SIZES = (Q_LORA, KV_LORA, ROPE_D, H_B * HD_B, H_B * HD_B, H_B * HD_B, H_IDX * D_IDX, D_IDX, H_IDX, S5_CH)

LANE = 128
BF = jnp.bfloat16
F32 = jnp.float32
NEG = -1e30
INT_MIN = -2 ** 31
VMEM_LIMIT = 56 * 1024 * 1024
CAUSAL_SEG = 256
KEY_BLOCK = 256
KEY_OF_NEG_INF = -2139095041
NO_TIE_LIMIT = 2 ** 30
DECODE_BATCHES_PER_STEP = 4

SEG_CQ, SEG_CKV, SEG_KR, SEG_BQ, SEG_BK, SEG_BV, SEG_IQ, SEG_IK, SEG_IW, SEG_U = (
    0, 256, 384, 512, 896, 1280, 1664, 1920, 2048, 2176)
IN_COLS_PADDED = 2432


def _dot(a, b):
    return jnp.dot(a.astype(BF), b.astype(BF), preferred_element_type=F32)


def _dot_nt(a, b):
    return lax.dot_general(a.astype(BF), b.astype(BF), (((1,), (1,)), ((), ())), preferred_element_type=F32)


def _rs(x, n):
    return lax.rsqrt(jnp.sum(x * x, axis=-1, keepdims=True) * (1.0 / n) + EPS)


def _half_rs(x, lo, n_lo, n_hi):
    sq = x * x
    sq_hi = sq.astype(BF)
    sq_lo = (sq - sq_hi.astype(F32)).astype(BF)
    row_lo = lax.broadcasted_iota(jnp.int32, (LANE, LANE), 0) < (LANE // 2)
    col_lo = lax.broadcasted_iota(jnp.int32, (LANE, LANE), 1) < (LANE // 2)
    same_half = jnp.where(row_lo == col_lo, 1.0, 0.0).astype(BF)
    sums = (jnp.dot(sq_hi, same_half, preferred_element_type=F32)
            + jnp.dot(sq_lo, same_half, preferred_element_type=F32))
    return lax.rsqrt(sums * jnp.where(lo, 1.0 / n_lo, 1.0 / n_hi) + EPS)


def _rope(x, tab_ref):
    return x * tab_ref[0] + pltpu.roll(x, LANE - ROPE_D // 2, 1) * tab_ref[1] + pltpu.roll(x, ROPE_D // 2, 1) * tab_ref[2]


def _params(sem):
    return pltpu.CompilerParams(dimension_semantics=sem, vmem_limit_bytes=VMEM_LIMIT)


def _ada_body(c_ref, w_ref, b_ref, o_ref):
    c = c_ref[...]
    s = c * (1.0 / (1.0 + jnp.exp(-c)))
    o_ref[0] = _dot(s, w_ref[0]) + b_ref[0]


def _ada(c_all, ada_w, ada_b):
    n = c_all.shape[0]
    tn = 1536
    return pl.pallas_call(
        _ada_body,
        grid=(DEPTH, 6 * D_MODEL // tn),
        in_specs=[pl.BlockSpec((n, D_MODEL), lambda l, j: (0, 0)),
                  pl.BlockSpec((1, D_MODEL, tn), lambda l, j: (l, 0, j)),
                  pl.BlockSpec((1, 1, tn), lambda l, j: (l, 0, j))],
        out_specs=pl.BlockSpec((1, n, tn), lambda l, j: (l, 0, j)),
        out_shape=jax.ShapeDtypeStruct((DEPTH, n, 6 * D_MODEL), F32),
        compiler_params=_params(("arbitrary", "arbitrary")),
        name="ada_mod",
    )(c_all, ada_w, ada_b.reshape(DEPTH, 1, 6 * D_MODEL))


def _proj_body(kv_transposed, x_ref, sc_ref, sh_ref, gn_ref, win_ref, gq_ref, wuq_ref, gqs_ref, tq_ref, gkv_ref, gkr_ref,
               tk_ref, gdq_ref, gdk_ref, tile_ref, *rest):
    q_o, ckv_o, kr_o, dq_o, dk_o, dkb_o, dv_o, dvb_o, iq_o, ik_o, ikt_o, iw_o, u_o = rest[-13:]
    x = x_ref[0]
    tm = x.shape[0]
    xn = x * _rs(x, D_MODEL)
    h = xn * gn_ref[...] * (1.0 + sc_ref[0]) + sh_ref[0]
    z = _dot(h, win_ref[...])
    lo = lax.broadcasted_iota(jnp.int32, (tm, LANE), 1) < (LANE // 2)

    cq = z[:, SEG_CQ:SEG_CQ + Q_LORA]
    cqn = cq * _rs(cq, Q_LORA) * gq_ref[...]
    q = _dot(cqn, wuq_ref[...])
    for hh in range(H_A):
        qs = q[:, hh * LANE:(hh + 1) * LANE]
        qn = qs * _half_rs(qs, lo, NOPE_D, ROPE_D) * gqs_ref[...]
        q_o[0, :, hh * LANE:(hh + 1) * LANE] = (_rope(qn, tq_ref) * (QK_D ** -0.5)).astype(BF)

    ckv = z[:, SEG_CKV:SEG_CKV + KV_LORA]
    ckv_o[0] = ckv * _rs(ckv, KV_LORA) * gkv_ref[...]
    krs = z[:, SEG_KR:SEG_KR + LANE]
    krn = krs * _rs(krs, ROPE_D) * gkr_ref[...]
    kr_o[0] = _rope(krn, tk_ref)[:, 0:ROPE_D]

    for j in range(H_B // 2):
        bq = z[:, SEG_BQ + j * LANE:SEG_BQ + (j + 1) * LANE]
        qn = bq * _half_rs(bq, lo, HD_B, HD_B) * gdq_ref[...] * (HD_B ** -0.5)
        bk = z[:, SEG_BK + j * LANE:SEG_BK + (j + 1) * LANE]
        kn = bk * _half_rs(bk, lo, HD_B, HD_B) * gdk_ref[...]
        bv = z[:, SEG_BV + j * LANE:SEG_BV + (j + 1) * LANE]
        dkb_o[0, :, j * LANE:(j + 1) * LANE] = kn.astype(BF)
        if kv_transposed:
            bvt = jnp.transpose(bv)
            dq_o[0, j * LANE:(j + 1) * LANE, :] = jnp.transpose(qn).astype(BF)
            dk_o[0, 0, j * LANE:(j + 1) * LANE, :] = jnp.transpose(kn)
            dv_o[0, 0, j * LANE:(j + 1) * LANE, :] = bvt
            dvb_o[0, j * LANE:(j + 1) * LANE, :] = bvt.astype(BF)
        else:
            dq_o[0, :, j * LANE:(j + 1) * LANE] = qn.astype(BF)
            dk_o[0, :, j * LANE:(j + 1) * LANE] = kn
            dv_o[0, :, j * LANE:(j + 1) * LANE] = bv
            dvb_o[0, :, j * LANE:(j + 1) * LANE] = bv.astype(BF)
    iqs = z[:, SEG_IQ:SEG_IQ + H_IDX * D_IDX] * (D_IDX ** -0.5)
    ik = z[:, SEG_IK:SEG_IK + D_IDX]
    iw = z[:, SEG_IW:SEG_IW + LANE] * (H_IDX ** -0.5)
    ik_o[0] = ik
    if kv_transposed:
        for c in range(H_IDX * D_IDX // LANE):
            iq_o[0, c * LANE:(c + 1) * LANE, :] = jnp.transpose(iqs[:, c * LANE:(c + 1) * LANE]).astype(BF)
        ikt_o[0] = ik.astype(BF)
        iw_o[0] = jnp.transpose(iw)[0:H_IDX, :]
    else:
        iq_o[0] = iqs.astype(BF)
        ikt_o[0] = _dot(ik, tile_ref[...]).astype(BF)
        iw_o[0] = iw
    u_o[0] = z[:, SEG_U:SEG_U + S5_CH]


def _proj(x, sc, sh, tabq, tabk, w, tm, kv_stack=None, layer=0):
    g, r, _ = x.shape
    per_token = sc.shape[1] != 1
    mod_spec = (pl.BlockSpec((1, tm, D_MODEL), lambda b, i: (b, i, 0)) if per_token
                else pl.BlockSpec((1, 1, D_MODEL), lambda b, i: (b, 0, 0)))
    const = lambda shape: pl.BlockSpec(shape, lambda b, i: tuple(0 for _ in shape))
    tab_spec = pl.BlockSpec((3, tm, LANE), lambda b, i: (0, i, 0))
    rows = lambda n, dt: (pl.BlockSpec((1, tm, n), lambda b, i: (b, i, 0)), jax.ShapeDtypeStruct((g, r, n), dt))
    cols = lambda n, dt: (pl.BlockSpec((1, n, tm), lambda b, i: (b, 0, i)), jax.ShapeDtypeStruct((g, n, r), dt))
    kv = 384
    nidx = H_IDX * D_IDX
    if kv_stack is None:
        dq, dk, dv, dvb = rows(kv, BF), rows(kv, F32), rows(kv, F32), rows(kv, BF)
        iq, ikt, iw = rows(nidx, BF), rows(nidx, BF), rows(LANE, F32)
    else:
        stack = (pl.BlockSpec((1, 1, kv, tm), lambda b, i: (layer, b, 0, i)),
                 jax.ShapeDtypeStruct((DEPTH, g, kv, r), F32))
        dq, dk, dv, dvb = cols(kv, BF), stack, stack, cols(kv, BF)
        iq, ikt, iw = cols(nidx, BF), rows(D_IDX, BF), cols(H_IDX, F32)
    outs = [rows(H_A * LANE, BF), rows(KV_LORA, F32), rows(ROPE_D, F32), dq, dk, rows(kv, BF), dv, dvb,
            iq, rows(D_IDX, F32), ikt, iw, rows(S5_CH, F32)]
    in_specs = [pl.BlockSpec((1, tm, D_MODEL), lambda b, i: (b, i, 0)), mod_spec, mod_spec,
                const((1, D_MODEL)), const((D_MODEL, IN_COLS_PADDED)), const((1, Q_LORA)),
                const((Q_LORA, H_A * LANE)), const((1, LANE)), tab_spec, const((1, KV_LORA)), const((1, LANE)),
                tab_spec, const((1, LANE)), const((1, LANE)), const((D_IDX, H_IDX * D_IDX))]
    args = [x, sc, sh, w["norm1_g"], w["w_in"], w["mla_gq"], w["mla_wuq"], w["mla_gq_slot"], tabq, w["mla_gkv"],
            w["mla_gkr"], tabk, w["dsa_gq"], w["dsa_gk"], w["idx_tile"]]
    aliases = {}
    if kv_stack is not None:
        aliases = {len(args): 4, len(args) + 1: 6}
        in_specs += [pl.BlockSpec(memory_space=pl.ANY), pl.BlockSpec(memory_space=pl.ANY)]
        args += list(kv_stack)
    return pl.pallas_call(
        functools.partial(_proj_body, kv_stack is not None),
        grid=(g, r // tm),
        in_specs=in_specs,
        out_specs=[o[0] for o in outs],
        out_shape=[o[1] for o in outs],
        input_output_aliases=aliases,
        compiler_params=_params(("arbitrary", "arbitrary")),
        name="in_proj",
    )(*args)


def _up_project(ckv, kr, wk_ref, wv_ref, gkn_ref, place_ref):
    krp = jnp.dot(kr, place_ref[...], precision=lax.Precision.HIGHEST, preferred_element_type=F32)
    kn = _dot(ckv, wk_ref[...])
    slots = []
    for hh in range(H_A):
        ks = kn[:, hh * LANE:(hh + 1) * LANE]
        slots.append((ks * _rs(ks, NOPE_D) * gkn_ref[...] + krp).astype(BF))
    return slots, _dot(ckv, wv_ref[...])


def _kvup_body(v_transposed, ckv_ref, kr_ref, wk_ref, wv_ref, gkn_ref, place_ref, k_o, v_o):
    slots, v = _up_project(ckv_ref[0], kr_ref[...], wk_ref, wv_ref, gkn_ref, place_ref)
    for hh in range(H_A):
        k_o[:, hh * LANE:(hh + 1) * LANE] = slots[hh]
    if v_transposed:
        for j in range(H_A * V_D // LANE):
            v_o[0, j * LANE:(j + 1) * LANE, :] = jnp.transpose(v[:, j * LANE:(j + 1) * LANE]).astype(BF)
    else:
        v_o[...] = v.astype(BF)


def _kvup(ckv, kr, w, tm, layer=0, batch_len=None):
    n = ckv.shape[1]
    const = lambda shape: pl.BlockSpec(shape, lambda i: tuple(0 for _ in shape))
    nv = H_A * V_D
    if batch_len is None:
        v_spec, v_shape = pl.BlockSpec((tm, nv), lambda i: (i, 0)), jax.ShapeDtypeStruct((n, nv), BF)
    else:
        per = batch_len // tm
        v_spec = pl.BlockSpec((1, nv, tm), lambda i: (i // per, 0, i % per))
        v_shape = jax.ShapeDtypeStruct((n // batch_len, nv, batch_len), BF)
    return pl.pallas_call(
        functools.partial(_kvup_body, batch_len is not None),
        grid=(n // tm,),
        in_specs=[pl.BlockSpec((1, tm, KV_LORA), lambda i: (layer, i, 0)), pl.BlockSpec((tm, ROPE_D), lambda i: (i, 0)),
                  const((KV_LORA, H_A * LANE)), const((KV_LORA, nv)), const((1, LANE)), const((ROPE_D, LANE))],
        out_specs=[pl.BlockSpec((tm, H_A * LANE), lambda i: (i, 0)), v_spec],
        out_shape=[jax.ShapeDtypeStruct((n, H_A * LANE), BF), v_shape],
        compiler_params=_params(("arbitrary",)),
        name="mla_kv_up",
    )(ckv, kr, w["mla_wk"], w["mla_wv"], w["mla_gkn_slot"], w["kr_place"])


def _visible(tq, sp, q0, s_valid):
    qpos = q0 + lax.broadcasted_iota(jnp.int32, (tq, sp), 0)
    kpos = lax.broadcasted_iota(jnp.int32, (tq, sp), 1)
    vis = lax.shift_right_logical(kpos, CHUNK_SHIFT) <= lax.shift_right_logical(qpos, CHUNK_SHIFT)
    if s_valid < sp:
        vis = jnp.logical_and(vis, kpos < s_valid)
    return vis, qpos, kpos


def _causal_variants(t, tq, compute):
    seg = min(CAUSAL_SEG, t)
    i = pl.program_id(1)
    for sg in range(t // seg):
        first, last = sg * seg // tq, (sg + 1) * seg // tq
        pl.when(jnp.logical_and(i >= first, i < last))(functools.partial(compute, (sg + 1) * seg))


def _mla_decode_body(p, t, q_ref, k_ref, v_ref, ckvp_ref, krp_ref, wk_ref, wv_ref, gkn_ref, place_ref, o_ref, kall, vall):
    sp = kall.shape[0]
    slots, v = _up_project(ckvp_ref[0], krp_ref[...], wk_ref, wv_ref, gkn_ref, place_ref)
    for hh in range(H_A):
        kall[0:p, hh * LANE:(hh + 1) * LANE] = slots[hh]
    vall[0:p, :] = v.astype(BF)
    kall[p:p + t, :] = k_ref[0]
    vall[p:p + t, :] = v_ref[0]
    if p + t < sp:
        kall[p + t:sp, :] = jnp.zeros((sp - p - t, kall.shape[1]), BF)
        vall[p + t:sp, :] = jnp.zeros((sp - p - t, vall.shape[1]), BF)
    _mla_compute(p, t, t, q_ref, lambda n: kall[...], lambda n: vall[...], o_ref, sp)


def _mla_compute(p, t, tq, q_ref, load_k, load_v, o_ref, svis):
    k = load_k(svis)
    v = load_v(svis)
    q = q_ref[0]
    vis, _, _ = _visible(tq, svis, p + pl.program_id(1) * tq, p + t)
    lo = lax.broadcasted_iota(jnp.int32, (tq, LANE), 1) < (LANE // 2)
    for j in range(H_A // 2):
        vpair = v[:, j * LANE:(j + 1) * LANE]
        outs = []
        for e in range(2):
            hh = 2 * j + e
            s = _dot_nt(q[:, hh * LANE:(hh + 1) * LANE], k[:, hh * LANE:(hh + 1) * LANE])
            s = jnp.where(vis, s, NEG)
            pr = jnp.exp(s - jnp.max(s, axis=-1, keepdims=True))
            outs.append(_dot(pr, vpair) / jnp.sum(pr, axis=-1, keepdims=True))
        o_ref[0, :, j * LANE:(j + 1) * LANE] = jnp.where(lo, outs[0], outs[1]).astype(BF)


def _mla_decode(q, k, v, ckv_cache, kr_past, w, layer, p):
    b, t, _ = q.shape
    sp = -(-(p + t) // LANE) * LANE
    nv = H_A * V_D
    batch = lambda n, wd: pl.BlockSpec((1, n, wd), lambda bi, i: (bi, 0, 0))
    const = lambda shape: pl.BlockSpec(shape, lambda bi, i: tuple(0 for _ in shape))
    return pl.pallas_call(
        functools.partial(_mla_decode_body, p, t),
        grid=(b, 1),
        in_specs=[batch(t, H_A * LANE), batch(t, H_A * LANE), batch(t, nv),
                  pl.BlockSpec((1, p, KV_LORA), lambda bi, i: (layer, bi, 0)),
                  pl.BlockSpec((p, ROPE_D), lambda bi, i: (bi, 0)),
                  const((KV_LORA, H_A * LANE)), const((KV_LORA, nv)), const((1, LANE)), const((ROPE_D, LANE))],
        out_specs=batch(t, nv),
        out_shape=jax.ShapeDtypeStruct((b, t, nv), BF),
        scratch_shapes=[pltpu.VMEM((sp, H_A * LANE), BF), pltpu.VMEM((sp, nv), BF)],
        compiler_params=_params(("arbitrary", "arbitrary")),
        name="mla_decode",
    )(q, k, v, ckv_cache, kr_past, w["mla_wk"], w["mla_wv"], w["mla_gkn_slot"], w["kr_place"])


def _key_to_f32(key):
    return lax.bitcast_convert_type(jnp.where(key >= 0, key, key ^ 0x7FFFFFFF), F32)


def _kth_largest(count_ge, shape, kk):
    t0 = jnp.where(count_ge(jnp.zeros(shape, F32)) >= kk, 0, INT_MIN).astype(jnp.int32)

    def bit_step(i, tk):
        cand = tk + lax.shift_left(jnp.int32(1), 30 - i)
        return jnp.where(count_ge(_key_to_f32(cand)) >= kk, cand, tk)

    tk = lax.fori_loop(0, 31, bit_step, t0)
    return _key_to_f32(jnp.maximum(tk, KEY_OF_NEG_INF))


def _kth_largest_two_phase(count_ge_bf16, count_ge, shape, kk):
    def grid_f32(k16):
        bits16 = jnp.where(k16 >= 0, k16, k16 ^ 0x7FFF)
        return lax.bitcast_convert_type(lax.shift_left(bits16, 16), F32)

    t16 = jnp.where(count_ge_bf16(jnp.zeros(shape, F32)) >= kk, 0, -32768).astype(jnp.int32)

    def coarse_step(i, tk):
        cand = tk + lax.shift_left(jnp.int32(1), 14 - i)
        return jnp.where(count_ge_bf16(grid_f32(cand)) >= kk, cand, tk)

    t16 = lax.fori_loop(0, 15, coarse_step, t16)
    base = lax.shift_left(t16, 16) + jnp.where(t16 < 0, 0xFFFF, 0) - 0x8000

    def fine_step(i, off):
        cand = off + lax.shift_left(jnp.int32(1), 16 - i)
        return jnp.where(count_ge(_key_to_f32(base + cand)) >= kk, cand, off)

    off = lax.fori_loop(0, 17, fine_step, jnp.zeros(shape, jnp.int32))
    return _key_to_f32(jnp.maximum(base + off, KEY_OF_NEG_INF))


def _tie_index_limit(count, count_tied_below, thr, kk, n_keys):
    n_gt = count(lambda x: x > thr)
    need = kk - n_gt
    excess = (count(lambda x: x >= thr) - n_gt) > need

    def bisect():
        def step(i, lim):
            cand = lim + lax.shift_left(jnp.int32(1), (n_keys - 1).bit_length() - 1 - i)
            return jnp.where(count_tied_below(cand) < need, cand, lim)

        lim = lax.fori_loop(0, (n_keys - 1).bit_length(), step, jnp.zeros(thr.shape, jnp.int32))
        return jnp.where(excess, lim, NO_TIE_LIMIT)

    any_excess = jnp.max(jnp.where(excess, 1.0, 0.0)) > 0.5
    return lax.cond(any_excess, bisect, lambda: jnp.full(thr.shape, NO_TIE_LIMIT, jnp.int32))


def _visible_block(kb, q0, tq, svis):
    if (kb + 1) * KEY_BLOCK <= svis - CAUSAL_SEG:
        return None
    kpos = kb * KEY_BLOCK + lax.broadcasted_iota(jnp.int32, (KEY_BLOCK, tq), 0)
    qpos = q0 + lax.broadcasted_iota(jnp.int32, (KEY_BLOCK, tq), 1)
    return lax.shift_right_logical(kpos, CHUNK_SHIFT) <= lax.shift_right_logical(qpos, CHUNK_SHIFT)


def _block_max(m, s):
    return jnp.maximum(m, jnp.max(s.reshape(KEY_BLOCK // 8, 8, s.shape[1]), axis=0))


def _softmax_pv(n_heads, m_acc, s_ref, p_ref, vt_ref, o_ref, svis):
    tq = p_ref.shape[1]
    nkb = svis // KEY_BLOCK
    for j in range(n_heads // 2):
        halves = []
        for e in range(2):
            hh = 2 * j + e
            m = jnp.max(m_acc[hh], axis=0, keepdims=True)
            l_acc = jnp.zeros((4, 8, tq), F32)
            for kb in range(nkb):
                rows = slice(kb * KEY_BLOCK, (kb + 1) * KEY_BLOCK)
                pr = jnp.exp(s_ref[hh, rows, :] - m)
                l_acc = l_acc + jnp.sum(pr.reshape(4, KEY_BLOCK // 32, 8, tq), axis=1)
                p_ref[rows, :] = pr.astype(BF)
            out_t = jnp.dot(vt_ref[0, j * LANE:(j + 1) * LANE, 0:svis], p_ref[0:svis, :], preferred_element_type=F32)
            out_t = out_t / jnp.sum(jnp.sum(l_acc, axis=0), axis=0, keepdims=True)
            halves.append(out_t[e * (LANE // 2):(e + 1) * (LANE // 2), :])
        o_ref[0, :, j * LANE:(j + 1) * LANE] = jnp.transpose(jnp.concatenate(halves, axis=0)).astype(BF)


def _mla_t_body(t, tq, q_ref, k_ref, vt_ref, o_ref, s_ref, p_ref):
    _causal_variants(t, tq, functools.partial(_mla_t_compute, tq, q_ref, k_ref, vt_ref, o_ref, s_ref, p_ref))


def _mla_t_compute(tq, q_ref, k_ref, vt_ref, o_ref, s_ref, p_ref, svis):
    q0 = pl.program_id(1) * tq
    q = q_ref[0]
    m_acc = [jnp.full((8, tq), NEG, F32) for _ in range(H_A)]
    for kb in range(svis // KEY_BLOCK):
        rows = slice(kb * KEY_BLOCK, (kb + 1) * KEY_BLOCK)
        vis = _visible_block(kb, q0, tq, svis)
        for hh in range(H_A):
            s = _dot_nt(k_ref[0, rows, hh * LANE:(hh + 1) * LANE], q[:, hh * LANE:(hh + 1) * LANE])
            if vis is not None:
                s = jnp.where(vis, s, NEG)
            s_ref[hh, rows, :] = s
            m_acc[hh] = _block_max(m_acc[hh], s)
    _softmax_pv(H_A, m_acc, s_ref, p_ref, vt_ref, o_ref, svis)


def _mla_prompt(q, k, vt, tq):
    b, t, _ = q.shape
    return pl.pallas_call(
        functools.partial(_mla_t_body, t, tq),
        grid=(b, t // tq),
        in_specs=[pl.BlockSpec((1, tq, H_A * LANE), lambda bi, i: (bi, i, 0)),
                  pl.BlockSpec((1, t, H_A * LANE), lambda bi, i: (bi, 0, 0)),
                  pl.BlockSpec((1, H_A * V_D, t), lambda bi, i: (bi, 0, 0))],
        out_specs=pl.BlockSpec((1, tq, H_A * V_D), lambda bi, i: (bi, i, 0)),
        out_shape=jax.ShapeDtypeStruct((b, t, H_A * V_D), BF),
        scratch_shapes=[pltpu.VMEM((H_A, t, tq), F32), pltpu.VMEM((t, tq), BF)],
        compiler_params=_params(("arbitrary", "arbitrary")),
        name="mla_prompt",
    )(q, k, vt)


def _dsa_t_body(t, tq, top, qt_ref, iqt_ref, iwt_ref, k_ref, vt_ref, ik_ref, o_ref, sc_ref, sc16_ref, s_ref, p_ref):
    _causal_variants(t, tq, functools.partial(_dsa_t_compute, tq, top, qt_ref, iqt_ref, iwt_ref, k_ref, vt_ref, ik_ref,
                                              o_ref, sc_ref, sc16_ref, s_ref, p_ref))


def _dsa_t_compute(tq, top, qt_ref, iqt_ref, iwt_ref, k_ref, vt_ref, ik_ref, o_ref, sc_ref, sc16_ref, s_ref, p_ref,
                   svis):
    q0 = pl.program_id(1) * tq
    nkb = svis // KEY_BLOCK

    iqt = iqt_ref[0]
    iwt = iwt_ref[0]
    for kb in range(nkb):
        rows = slice(kb * KEY_BLOCK, (kb + 1) * KEY_BLOCK)
        ikb = ik_ref[0, rows, :]
        acc = jnp.zeros((KEY_BLOCK, tq), F32)
        for hh in range(H_IDX):
            dots = jnp.dot(ikb, iqt[hh * D_IDX:(hh + 1) * D_IDX, :], preferred_element_type=F32)
            acc = acc + iwt[hh:hh + 1, :] * jnp.maximum(dots, 0.0)
        vis = _visible_block(kb, q0, tq, svis)
        if vis is not None:
            acc = jnp.where(vis, acc, -jnp.inf)
        sc_ref[rows, :] = acc
        sc16_ref[rows, :] = acc.astype(BF)

    def count(pred):
        x = sc_ref[0:svis, :].reshape(8, svis // 64, 8, tq)
        part = jnp.sum(jnp.where(pred(x), 1.0, 0.0), axis=1)
        return jnp.sum(jnp.sum(part, axis=0), axis=0, keepdims=True)

    def count_ge(cand):
        return count(lambda x: x >= cand)

    def count_ge_bf16(cand):
        x = sc16_ref[0:svis, :].reshape(8, svis // 128, 16, tq)
        c = cand.astype(BF)
        one, zero = jnp.ones((8, 16, tq), BF), jnp.zeros((8, 16, tq), BF)
        part = zero
        for i in range(svis // 128):
            part = part + jnp.where(x[:, i] >= c, one, zero)
        return jnp.sum(jnp.sum(part.astype(F32), axis=0), axis=0, keepdims=True)

    thr = _kth_largest_two_phase(count_ge_bf16, count_ge, (1, tq), float(top))

    def count_tied_below(limit):
        acc = jnp.zeros((8, tq), F32)
        for kb in range(nkb):
            x = sc_ref[kb * KEY_BLOCK:(kb + 1) * KEY_BLOCK, :]
            kpos = kb * KEY_BLOCK + lax.broadcasted_iota(jnp.int32, (KEY_BLOCK, tq), 0)
            hit = jnp.where(x == thr, jnp.where(kpos < limit, 1.0, 0.0), 0.0)
            acc = acc + jnp.sum(hit.reshape(KEY_BLOCK // 8, 8, tq), axis=0)
        return jnp.sum(acc, axis=0, keepdims=True)

    tie_limit = _tie_index_limit(count, count_tied_below, thr, float(top), svis)

    qt = qt_ref[0]
    zeros = jnp.zeros((HD_B, tq), BF)
    qms = []
    for hh in range(H_B):
        qh = qt[hh * HD_B:(hh + 1) * HD_B, :]
        qms.append(jnp.concatenate([qh, zeros] if hh % 2 == 0 else [zeros, qh], axis=0))
    m_acc = [jnp.full((8, tq), NEG, F32) for _ in range(H_B)]
    qpos = q0 + lax.broadcasted_iota(jnp.int32, (KEY_BLOCK, tq), 1)
    for kb in range(nkb):
        rows = slice(kb * KEY_BLOCK, (kb + 1) * KEY_BLOCK)
        kpos = kb * KEY_BLOCK + lax.broadcasted_iota(jnp.int32, (KEY_BLOCK, tq), 0)
        x = sc_ref[rows, :]
        bias = jnp.where(x > thr, 0.0, jnp.where(x == thr, jnp.where(kpos <= tie_limit, 0.0, NEG), NEG))
        vis = _visible_block(kb, q0, tq, svis)
        if vis is not None:
            bias = jnp.where(vis, bias, NEG)
        dist = jnp.abs(qpos - kpos).astype(F32)
        for hh in range(H_B):
            kpair = k_ref[0, rows, (hh // 2) * LANE:(hh // 2 + 1) * LANE]
            s = jnp.dot(kpair, qms[hh], preferred_element_type=F32) - (2.0 ** (-8.0 * (hh + 1) / H_B)) * dist + bias
            s_ref[hh, rows, :] = s
            m_acc[hh] = _block_max(m_acc[hh], s)
    _softmax_pv(H_B, m_acc, s_ref, p_ref, vt_ref, o_ref, svis)


def _dsa_prompt(qt, iqt, iwt, k, vt, ik, tq):
    b, _, t = qt.shape
    top = min(TOPK_MAX, t // 4)
    qcols = lambda n: pl.BlockSpec((1, n, tq), lambda bi, i: (bi, 0, i))
    batch = lambda n, wd: pl.BlockSpec((1, n, wd), lambda bi, i: (bi, 0, 0))
    return pl.pallas_call(
        functools.partial(_dsa_t_body, t, tq, top),
        grid=(b, t // tq),
        in_specs=[qcols(384), qcols(H_IDX * D_IDX), qcols(H_IDX), batch(t, 384), batch(384, t), batch(t, D_IDX)],
        out_specs=pl.BlockSpec((1, tq, 384), lambda bi, i: (bi, i, 0)),
        out_shape=jax.ShapeDtypeStruct((b, t, 384), BF),
        scratch_shapes=[pltpu.VMEM((t, tq), F32), pltpu.VMEM((t, tq), BF), pltpu.VMEM((H_B, t, tq), F32),
                        pltpu.VMEM((t, tq), BF)],
        compiler_params=_params(("arbitrary", "arbitrary")),
        name="dsa_attn",
    )(qt, iqt, iwt, k, vt, ik)


def _dsa_s_body(p, t, top, q_ref, iq_ref, iw_ref, k_ref, v_ref, ikt_ref, kpt_ref, vpt_ref, ikpt_ref, o_ref, sc_ref):
    nb = q_ref.shape[0]
    sp = sc_ref.shape[1]
    pad_rows = lambda x: jnp.concatenate([x, jnp.zeros((LANE - t, x.shape[1]), x.dtype)], axis=0)
    lane_b = lax.broadcasted_iota(jnp.int32, (t, sp), 1)
    qrow_b = p + lax.broadcasted_iota(jnp.int32, (t, sp), 0)
    valid_b = jnp.logical_and(
        lane_b < p + t, lax.shift_right_logical(lane_b, CHUNK_SHIFT) <= lax.shift_right_logical(qrow_b, CHUNK_SHIFT))
    lane = lax.broadcasted_iota(jnp.int32, (nb * t, sp), 1)
    valid = jnp.concatenate([valid_b] * nb, axis=0)

    head_of_lane = lax.shift_right_logical(lax.broadcasted_iota(jnp.int32, (t, H_IDX * D_IDX), 1), 5)
    for bb in range(nb):
        iq = iq_ref[bb]
        iw = iw_ref[bb]
        ikp = ikpt_ref[0, bb].astype(BF)
        ikp_tiled = jnp.concatenate([ikp] * H_IDX, axis=0)
        ikn = pad_rows(ikt_ref[bb])
        sc_p = jnp.zeros((t, p), F32)
        sc_n = jnp.zeros((t, LANE), F32)
        for hh in range(H_IDX):
            qh = jnp.where(head_of_lane == hh, iq, jnp.zeros_like(iq))
            wh = iw[:, hh:hh + 1]
            sc_p = sc_p + wh * jnp.maximum(jnp.dot(qh, ikp_tiled, preferred_element_type=F32), 0.0)
            sc_n = sc_n + wh * jnp.maximum(_dot_nt(qh, ikn), 0.0)
        sc_ref[bb * t:(bb + 1) * t, :] = jnp.where(valid_b, jnp.concatenate([sc_p, sc_n], axis=1), -jnp.inf)

    def count(pred):
        return jnp.sum(jnp.where(pred(sc_ref[...]), 1.0, 0.0), axis=1, keepdims=True)

    thr = _kth_largest(lambda cand: count(lambda x: x >= cand), (nb * t, 1), float(top))

    def count_tied_below(limit):
        hit = jnp.where(sc_ref[...] == thr, jnp.where(lane < limit, 1.0, 0.0), 0.0)
        return jnp.sum(hit, axis=1, keepdims=True)

    tie_limit = _tie_index_limit(count, count_tied_below, thr, float(top), sp)
    x = sc_ref[...]
    bias_all = jnp.where(x > thr, 0.0, jnp.where(x == thr, jnp.where(lane <= tie_limit, 0.0, NEG), NEG))
    bias_all = jnp.where(valid, bias_all, NEG)
    dist = jnp.abs(qrow_b - lane_b).astype(F32)

    lo = lax.broadcasted_iota(jnp.int32, (t, LANE), 1) < (LANE // 2)
    for bb in range(nb):
        q = q_ref[bb]
        bias = bias_all[bb * t:(bb + 1) * t, :]
        for j in range(H_B // 2):
            cols = slice(j * LANE, (j + 1) * LANE)
            qpair = q[:, cols]
            kpt = kpt_ref[0, bb, cols, :].astype(BF)
            vpt = vpt_ref[0, bb, cols, :].astype(BF)
            kn = pad_rows(k_ref[bb][:, cols])
            vn = pad_rows(v_ref[bb][:, cols])
            outs = []
            for e in range(2):
                hh = 2 * j + e
                qh = jnp.where(lo if e == 0 else jnp.logical_not(lo), qpair, jnp.zeros_like(qpair))
                s = jnp.concatenate([jnp.dot(qh, kpt, preferred_element_type=F32), _dot_nt(qh, kn)], axis=1)
                s = s - (2.0 ** (-8.0 * (hh + 1) / H_B)) * dist + bias
                pr = jnp.exp(s - jnp.max(s, axis=-1, keepdims=True))
                prb = pr.astype(BF)
                o = _dot_nt(prb[:, 0:p], vpt) + jnp.dot(prb[:, p:sp], vn, preferred_element_type=F32)
                outs.append(o / jnp.sum(pr, axis=-1, keepdims=True))
            o_ref[bb, :, cols] = jnp.where(lo, outs[0], outs[1]).astype(BF)


def _dsa_decode(q, iq, iw, k, v, ikt, kpt, vpt, ikpt, layer):
    b, t, _ = q.shape
    p = kpt.shape[-1]
    sp = p + LANE
    top = min(TOPK_MAX, (p + t) // 4)
    nb = math.gcd(b, DECODE_BATCHES_PER_STEP)
    new = lambda wd: pl.BlockSpec((nb, t, wd), lambda bi: (bi, 0, 0))
    cache = lambda n: pl.BlockSpec((1, nb, n, p), lambda bi: (layer, bi, 0, 0))
    return pl.pallas_call(
        functools.partial(_dsa_s_body, p, t, top),
        grid=(b // nb,),
        in_specs=[new(384), new(256), new(LANE), new(384), new(384), new(256), cache(384), cache(384), cache(D_IDX)],
        out_specs=new(384),
        out_shape=jax.ShapeDtypeStruct((b, t, 384), BF),
        scratch_shapes=[pltpu.VMEM((nb * t, sp), F32)],
        compiler_params=_params(("arbitrary",)),
        name="dsa_decode",
    )(q, iq, iw, k, v, ikt, kpt, vpt, ikpt)


def _s5_body(has_h0, nb, tt, u_ref, are_ref, aim_ref, ldt_ref, bre_ref, bim_ref, cre_ref, cim_ref, d_ref, wg_ref,
             bg_ref, *rest):
    if has_h0:
        h0re_ref, h0im_ref, o_ref, sre_o, sim_o, xre, xim = rest
    else:
        o_ref, sre_o, sim_o, xre, xim = rest

    @pl.when(pl.program_id(0) == 0)
    def _():
        if has_h0:
            sre_o[...] = h0re_ref[...]
            sim_o[...] = h0im_ref[...]
        else:
            sre_o[...] = jnp.zeros_like(sre_o)
            sim_o[...] = jnp.zeros_like(sim_o)

    ar = are_ref[...]
    ai = aim_ref[...]
    dt = jnp.exp(ldt_ref[...])
    mag = jnp.exp(dt * ar)
    ab_re = mag * jnp.cos(dt * ai)
    ab_im = mag * jnp.sin(dt * ai)
    den = ar * ar + ai * ai
    nr = ab_re - 1.0
    f_re = (nr * ar + ab_im * ai) / den
    f_im = (ab_im * ar - nr * ai) / den

    u = u_ref[...].reshape(tt * nb, S5_CH)
    bu_re = _dot(u, bre_ref[...])
    bu_im = _dot(u, bim_ref[...])
    xre[...] = f_re * bu_re - f_im * bu_im
    xim[...] = f_re * bu_im + f_im * bu_re

    a_re = jnp.broadcast_to(ab_re, (nb, S5_N))
    a_im = jnp.broadcast_to(ab_im, (nb, S5_N))

    def step(ti, carry):
        s_re, s_im = carry
        rows = pl.ds(pl.multiple_of(ti * nb, nb), nb)
        n_re = a_re * s_re - a_im * s_im + xre[rows, :]
        n_im = a_re * s_im + a_im * s_re + xim[rows, :]
        xre[rows, :] = n_re
        xim[rows, :] = n_im
        return n_re, n_im

    s_re, s_im = lax.fori_loop(0, tt, step, (sre_o[...], sim_o[...]), unroll=2)
    sre_o[...] = s_re
    sim_o[...] = s_im

    y = _dot(xre[...], cre_ref[...]) - _dot(xim[...], cim_ref[...]) + d_ref[...] * u
    g = _dot(y, wg_ref[...]) + bg_ref[...]
    out = g[:, 0:S5_CH] * (1.0 / (1.0 + jnp.exp(-g[:, S5_CH:2 * S5_CH])))
    o_ref[...] = out.reshape(tt, nb, S5_CH)


def _s5(u, h0, w, tt):
    nb, t, _ = u.shape
    u = jnp.swapaxes(u, 0, 1)
    const = lambda shape: pl.BlockSpec(shape, lambda i: tuple(0 for _ in shape))
    in_specs = [pl.BlockSpec((tt, nb, S5_CH), lambda i: (i, 0, 0)), const((1, S5_N)), const((1, S5_N)),
                const((1, S5_N)), const((S5_CH, S5_N)), const((S5_CH, S5_N)), const((S5_N, S5_CH)),
                const((S5_N, S5_CH)), const((1, S5_CH)), const((S5_CH, 2 * S5_CH)), const((1, 2 * S5_CH))]
    args = [u, w["s5_a_re"], w["s5_a_im"], w["s5_log_dt"], w["s5_bre"], w["s5_bim"], w["s5_cre"], w["s5_cim"],
            w["s5_d"], w["s5_w_glu"], w["s5_b_glu"]]
    if h0 is not None:
        in_specs += [const((nb, S5_N)), const((nb, S5_N))]
        args += [h0[0], h0[1]]
    out, s_re, s_im = pl.pallas_call(
        functools.partial(_s5_body, h0 is not None, nb, tt),
        grid=(t // tt,),
        in_specs=in_specs,
        out_specs=[pl.BlockSpec((tt, nb, S5_CH), lambda i: (i, 0, 0)), const((nb, S5_N)), const((nb, S5_N))],
        out_shape=[jax.ShapeDtypeStruct((t, nb, S5_CH), F32), jax.ShapeDtypeStruct((nb, S5_N), F32),
                   jax.ShapeDtypeStruct((nb, S5_N), F32)],
        scratch_shapes=[pltpu.VMEM((tt * nb, S5_N), F32), pltpu.VMEM((tt * nb, S5_N), F32)],
        compiler_params=_params(("arbitrary",)),
        name="s5_scan",
    )(*args)
    return jnp.swapaxes(out, 0, 1).astype(BF), s_re, s_im


def _out_body(a_ref, b_ref, c_ref, x_ref, g1_ref, sc_ref, sh_ref, g2_ref, gn_ref, wa_ref, wb_ref, wc_ref, w1_ref,
              w2_ref, o_ref):
    mix = _dot(a_ref[0], wa_ref[...]) + _dot(b_ref[0], wb_ref[...]) + _dot(c_ref[0], wc_ref[...])
    x1 = x_ref[0] + g1_ref[0] * mix
    h2 = (x1 * _rs(x1, D_MODEL) * gn_ref[...] * (1.0 + sc_ref[0]) + sh_ref[0]).astype(BF)
    ff = jnp.zeros_like(x1)
    for c in range(D_FF // D_MODEL):
        hid = jnp.maximum(_dot(h2, w1_ref[:, c * D_MODEL:(c + 1) * D_MODEL]), 0.0)
        ff = ff + _dot(hid * hid, w2_ref[c * D_MODEL:(c + 1) * D_MODEL, :])
    o_ref[0] = x1 + g2_ref[0] * ff


def _out(a, b, c, x, g1, sc, sh, g2, w, tm):
    g, r, _ = x.shape
    per_token = sc.shape[1] != 1
    mod_spec = (pl.BlockSpec((1, tm, D_MODEL), lambda bi, i: (bi, i, 0)) if per_token
                else pl.BlockSpec((1, 1, D_MODEL), lambda bi, i: (bi, 0, 0)))
    tile = lambda n: pl.BlockSpec((1, tm, n), lambda bi, i: (bi, i, 0))
    const = lambda shape: pl.BlockSpec(shape, lambda bi, i: tuple(0 for _ in shape), pipeline_mode=pl.Buffered(1))
    return pl.pallas_call(
        _out_body,
        grid=(g, r // tm),
        in_specs=[tile(384), tile(384), tile(S5_CH), tile(D_MODEL), mod_spec, mod_spec, mod_spec, mod_spec,
                  const((1, D_MODEL)), const((384, D_MODEL)), const((384, D_MODEL)), const((S5_CH, D_MODEL)),
                  const((D_MODEL, D_FF)), const((D_FF, D_MODEL))],
        out_specs=tile(D_MODEL),
        out_shape=jax.ShapeDtypeStruct((g, r, D_MODEL), F32),
        compiler_params=_params(("arbitrary", "arbitrary")),
        name="out_mlp",
    )(a, b, c, x, g1, sc, sh, g2, w["norm2_g"], w["w_out_a"], w["w_out_b"], w["w_out_c"], w["ff_w1"], w["ff_w2"])


def _slot_pad(v, lo, n):
    return jnp.pad(v, ((0, 0), (lo, LANE - lo - n)))


def _place_cols(wm, segments, total):
    parts, at = [], 0
    for dst, src, n in segments:
        if dst > at:
            parts.append(jnp.zeros(wm.shape[:-1] + (dst - at,), wm.dtype))
        parts.append(wm[..., src:src + n])
        at = dst + n
    if total > at:
        parts.append(jnp.zeros(wm.shape[:-1] + (total - at,), wm.dtype))
    return jnp.concatenate(parts, axis=-1)


def _prep_weights(p):
    f = {}
    offs = np.cumsum((0,) + IN_SIZES)
    starts = (SEG_CQ, SEG_CKV, SEG_KR, SEG_BQ, SEG_BK, SEG_BV, SEG_IQ, SEG_IK, SEG_IW, SEG_U)
    f["w_in"] = _place_cols(p["w_in"], [(s, int(o), n) for s, o, n in zip(starts, offs[:-1], IN_SIZES)],
                            IN_COLS_PADDED).astype(BF)
    kvw = NOPE_D + V_D
    f["mla_wuq"] = _place_cols(p["mla_wuq"], [(hh * LANE, hh * QK_D, QK_D) for hh in range(H_A)], H_A * LANE).astype(BF)
    f["mla_wk"] = _place_cols(p["mla_wukv"], [(hh * LANE, hh * kvw, NOPE_D) for hh in range(H_A)], H_A * LANE).astype(BF)
    f["mla_wv"] = _place_cols(p["mla_wukv"], [(hh * V_D, hh * kvw + NOPE_D, V_D) for hh in range(H_A)],
                              H_A * V_D).astype(BF)
    f["mla_gq_slot"] = jnp.concatenate([p["mla_gqn"], p["mla_gqr"], jnp.zeros((DEPTH, LANE - QK_D), F32)], axis=-1)
    f["mla_gkn_slot"] = _slot_pad(p["mla_gkn"], 0, NOPE_D)
    f["mla_gkr"] = _slot_pad(p["mla_gkr"], 0, ROPE_D)
    f["dsa_gq"] = jnp.concatenate([p["dsa_gq"], p["dsa_gq"]], axis=-1)
    f["dsa_gk"] = jnp.concatenate([p["dsa_gk"], p["dsa_gk"]], axis=-1)
    for name in ("norm1_g", "norm2_g", "mla_gq", "mla_gkv", "s5_d", "s5_b_glu"):
        f[name] = p[name]
    f["w_out_a"] = p["w_out"][:, 0:384].astype(BF)
    f["w_out_b"] = p["w_out"][:, 384:768].astype(BF)
    f["w_out_c"] = p["w_out"][:, 768:1024].astype(BF)
    f["ff_w1"] = p["ff_w1"].astype(BF)
    f["ff_w2"] = p["ff_w2"].astype(BF)
    f["s5_w_glu"] = p["s5_w_glu"].astype(BF)
    eye = jnp.eye(S5_G, dtype=F32)
    f["s5_bre"] = jnp.einsum("lgpc,gh->lgchp", p["s5_b_re"], eye).reshape(DEPTH, S5_CH, S5_N).astype(BF)
    f["s5_bim"] = jnp.einsum("lgpc,gh->lgchp", p["s5_b_im"], eye).reshape(DEPTH, S5_CH, S5_N).astype(BF)
    f["s5_cre"] = jnp.einsum("lgcp,gh->lgphc", p["s5_c_re"], eye).reshape(DEPTH, S5_N, S5_CH).astype(BF)
    f["s5_cim"] = jnp.einsum("lgcp,gh->lgphc", p["s5_c_im"], eye).reshape(DEPTH, S5_N, S5_CH).astype(BF)
    f["s5_a_re"] = p["s5_a_re"].reshape(DEPTH, S5_N)
    f["s5_a_im"] = p["s5_a_im"].reshape(DEPTH, S5_N)
    f["s5_log_dt"] = jnp.repeat(p["s5_log_dt"], S5_P, axis=-1)
    return f


def _layer_weights(f, l):
    w = {k: (v[l] if v.ndim == 3 else v[l][None, :]) for k, v in f.items()}
    place = np.zeros((ROPE_D, LANE), np.float32)
    place[np.arange(ROPE_D), NOPE_D + np.arange(ROPE_D)] = 1.0
    w["kr_place"] = jnp.asarray(place)
    tile = np.zeros((D_IDX, H_IDX * D_IDX), np.float32)
    for hh in range(H_IDX):
        tile[np.arange(D_IDX), hh * D_IDX + np.arange(D_IDX)] = 1.0
    w["idx_tile"] = jnp.asarray(tile, dtype=BF)
    return w


def _rope_tables(pos, lo):
    half = ROPE_D // 2
    inv = ROPE_BASE ** (-jnp.arange(half, dtype=F32) / half)
    ang = pos.astype(F32)[:, None] * inv[None, :]
    cos, sin = jnp.cos(ang), jnp.sin(ang)
    n = pos.shape[0]
    z = lambda w_: jnp.zeros((n, w_), F32)
    tail = LANE - lo - ROPE_D
    c = jnp.concatenate([jnp.ones((n, lo), F32), cos, cos, z(tail)], axis=-1)
    s1 = jnp.concatenate([z(lo), -sin, z(half), z(tail)], axis=-1)
    s2 = jnp.concatenate([z(lo), z(half), sin, z(tail)], axis=-1)
    return jnp.stack([c, s1, s2])


def _prompt_layer(x, mods, w, pos, tiles, layer, kv_stack):
    b, t, _ = x.shape
    tm, tq_mla, tq_dsa, tt = tiles
    sh1, sc1, g1, sh2, sc2, g2 = mods
    outs = _proj(x, sc1, sh1, _rope_tables(pos, NOPE_D), _rope_tables(pos, 0), w, tm, kv_stack, layer)
    q, ckv, kr, dqt, dk_stack, dkb, dv_stack, dvtb, iqt, ik, ikb, iwt, u = outs
    kn, vnt = _kvup(ckv.reshape(1, b * t, KV_LORA), kr.reshape(b * t, ROPE_D), w, min(512, t), batch_len=t)
    a_out = _mla_prompt(q, kn.reshape(b, t, H_A * LANE), vnt, tq_mla)
    b_out = _dsa_prompt(dqt, iqt, iwt, dkb, dvtb, ikb, tq_dsa)
    c_out, s_re, s_im = _s5(u, None, w, tt)
    y = _out(a_out, b_out, c_out, x, g1, sc2, sh2, g2, w, tm)
    state = (ckv, kr, None, None, ik, s_re.reshape(b, S5_G, S5_P), s_im.reshape(b, S5_G, S5_P))
    return y, state, (dk_stack, dv_stack)


def _decode_layer(x, mods, caches, w, pos, layer):
    b, t, _ = x.shape
    n = b * t
    c_ckv, c_kr, c_kt, c_vt, c_ikt, c_sre, c_sim = caches
    plen = c_kt.shape[-1]
    xf = x.reshape(1, n, D_MODEL)
    sh1, sc1, g1, sh2, sc2, g2 = (jnp.broadcast_to(m, (b, t, D_MODEL)).reshape(1, n, D_MODEL) for m in mods)
    tabq = jnp.tile(_rope_tables(pos, NOPE_D), (1, b, 1))
    tabk = jnp.tile(_rope_tables(pos, 0), (1, b, 1))
    outs = _proj(xf, sc1, sh1, tabq, tabk, w, n)
    q, ckv, kr, dq, dk, dkb, dv, dvb, iq, ik, ikt, iw, u = (o.reshape(b, t, o.shape[-1]) for o in outs)
    kn, vn = _kvup(ckv.reshape(1, n, KV_LORA), kr.reshape(n, ROPE_D), w, n)
    a_out = _mla_decode(q, kn.reshape(b, t, H_A * LANE), vn.reshape(b, t, H_A * V_D), c_ckv,
                        c_kr[layer].reshape(b * plen, ROPE_D), w, layer, plen)
    b_out = _dsa_decode(dq, iq, iw, dkb, dvb, ikt, c_kt, c_vt, c_ikt, layer)
    c_out, s_re, s_im = _s5(u, (c_sre[layer], c_sim[layer]), w, t)
    y = _out(a_out.reshape(1, n, -1), b_out.reshape(1, n, -1), c_out.reshape(1, n, -1), xf, g1, sc2, sh2, g2, w, n)
    state = (ckv, kr, dk.reshape(b, t, H_B, HD_B), dv.reshape(b, t, H_B, HD_B), ik,
             s_re.reshape(b, S5_G, S5_P), s_im.reshape(b, S5_G, S5_P))
    return y.reshape(b, t, D_MODEL), state


def kernel(x_prompt, x_sample, c_prompt, c_sample, cache_mla_ckv, cache_mla_krope, cache_dsa_k, cache_dsa_v, cache_dsa_idxk, state_s5_re, state_s5_im, ada_w, ada_b, norm1_g, norm2_g, w_in, w_out, mla_gq, mla_wuq, mla_gkv, mla_wukv, mla_gqn, mla_gqr, mla_gkn, mla_gkr, dsa_gq, dsa_gk, s5_a_re, s5_a_im, s5_b_re, s5_b_im, s5_c_re, s5_c_im, s5_d, s5_log_dt, s5_w_glu, s5_b_glu, ff_w1, ff_w2):
    params = dict(norm1_g=norm1_g, norm2_g=norm2_g, w_in=w_in, w_out=w_out, mla_gq=mla_gq, mla_wuq=mla_wuq,
                  mla_gkv=mla_gkv, mla_wukv=mla_wukv, mla_gqn=mla_gqn, mla_gqr=mla_gqr, mla_gkn=mla_gkn,
                  mla_gkr=mla_gkr, dsa_gq=dsa_gq, dsa_gk=dsa_gk, s5_a_re=s5_a_re, s5_a_im=s5_a_im, s5_b_re=s5_b_re,
                  s5_b_im=s5_b_im, s5_c_re=s5_c_re, s5_c_im=s5_c_im, s5_d=s5_d, s5_log_dt=s5_log_dt,
                  s5_w_glu=s5_w_glu, s5_b_glu=s5_b_glu, ff_w1=ff_w1, ff_w2=ff_w2)
    f = _prep_weights(params)
    nbp, tp, _ = x_prompt.shape
    nbs, ts, _ = x_sample.shape
    past_len = cache_mla_ckv.shape[2]
    mod = _ada(jnp.concatenate([c_prompt, c_sample], axis=0), ada_w, ada_b)
    pos_p = jnp.arange(tp, dtype=jnp.int32)
    pos_s = past_len + jnp.arange(ts, dtype=jnp.int32)
    caches = (cache_mla_ckv.reshape(DEPTH, nbs * past_len, KV_LORA), cache_mla_krope,
              jnp.transpose(cache_dsa_k, (0, 1, 3, 4, 2)).reshape(DEPTH, nbs, H_B * HD_B, past_len),
              jnp.transpose(cache_dsa_v, (0, 1, 3, 4, 2)).reshape(DEPTH, nbs, H_B * HD_B, past_len),
              jnp.transpose(cache_dsa_idxk, (0, 1, 3, 2)),
              state_s5_re.reshape(DEPTH, nbs, S5_N), state_s5_im.reshape(DEPTH, nbs, S5_N))
    xp, xs = x_prompt, x_sample
    new_p = [[] for _ in range(7)]
    new_s = [[] for _ in range(7)]
    kv_stack = tuple(jnp.zeros((DEPTH, nbp, H_B * HD_B, tp), F32) for _ in range(2))
    for l in range(DEPTH):
        w = _layer_weights(f, l)
        mods = [mod[l][:, None, i * D_MODEL:(i + 1) * D_MODEL] for i in range(6)]
        xp, st_p, kv_stack = _prompt_layer(xp, [m[:nbp] for m in mods], w, pos_p,
                                           (min(512, tp), min(128, tp), min(128, tp), min(128, tp)), l, kv_stack)
        xs, st_s = _decode_layer(xs, [m[nbp:] for m in mods], caches, w, pos_s, l)
        for i in range(7):
            new_p[i].append(st_p[i])
            new_s[i].append(st_s[i])
    outs = [xp, xs]
    for i in range(7):
        if i in (2, 3):
            stacked = jnp.transpose(kv_stack[i - 2].reshape(DEPTH, nbp, H_B, HD_B, tp), (0, 1, 4, 2, 3))
        else:
            stacked = jnp.stack(new_p[i])
        outs.append(stacked)
        outs.append(jnp.stack(new_s[i]))
    return tuple(outs)
```
